```python
import jax, jax.numpy as jnp
from jax import lax
import numpy as np

D_MODEL = 1024
BATCH = 8
SEQ = 4096
DEPTH = 1

CHUNK = 64
Q_BLOCK = 64
D_CONV = D_MODEL // 2
CONV_WIDTH = 31
N_HEADS = 8
HEAD_DIM = 64
D_ATTN = N_HEADS * HEAD_DIM
N_IDX_HEADS = 8
IDX_DIM = 64
TOPK_MAX = 256
D_FF = 4 * D_MODEL
N_BRANCHES = 2
ROPE_THETA = 10000.0
LN_EPS = 1e-5
NEG_INF = -1e30
DEEPNORM_ALPHA = (2.0 * DEPTH) ** 0.25
DEEPNORM_BETA = (8.0 * DEPTH) ** -0.25
IN_SPLIT_SIZES = (D_CONV, D_CONV, D_ATTN, D_ATTN, D_ATTN,
                  N_IDX_HEADS * IDX_DIM, IDX_DIM, N_IDX_HEADS, D_MODEL, D_MODEL)
D_IN = sum(IN_SPLIT_SIZES)

kernel_name = "hybrid_conformer_conv_dsa_gated_deepnorm"


def layer_norm(x, g, b):
    xf = x.astype(jnp.float32)
    mu = jnp.mean(xf, axis=-1, keepdims=True)
    var = jnp.mean(jnp.square(xf - mu), axis=-1, keepdims=True)
    return ((xf - mu) * lax.rsqrt(var + LN_EPS) * g.astype(jnp.float32)
            + b.astype(jnp.float32)).astype(x.dtype)


def rope(x, pos):
    d = x.shape[-1]
    inv_freq = ROPE_THETA ** (-jnp.arange(0, d, 2, dtype=jnp.float32) / d)
    ang = pos.astype(jnp.float32)[:, None] * inv_freq[None, :]
    cos = jnp.cos(ang)[:, None, :]
    sin = jnp.sin(ang)[:, None, :]
    xf = x.astype(jnp.float32)
    x1, x2 = xf[..., : d // 2], xf[..., d // 2:]
    return jnp.concatenate([x1 * cos - x2 * sin, x2 * cos + x1 * sin], axis=-1).astype(x.dtype)


def conformer_conv_branch(a, b, dw_w, dw_b, ln_g, ln_b, w_out):
    u = a * jax.nn.sigmoid(b)
    u = lax.conv_general_dilated(
        u, dw_w[:, None, :].astype(u.dtype), window_strides=(1,),
        padding=[(CONV_WIDTH - 1, 0)],
        dimension_numbers=('NWC', 'WIO', 'NWC'),
        feature_group_count=D_CONV) + dw_b
    u = jax.nn.silu(layer_norm(u, ln_g, ln_b))
    return u @ w_out


def dsa_sparse_attention(q, k, v, qi, ki, wi, pos):
    B, L = q.shape[0], q.shape[1]
    dt = q.dtype
    topk = min(TOPK_MAX, L // 4)
    n_blk = L // Q_BLOCK
    key_chunk = pos // CHUNK
    ki_f = ki.astype(jnp.float32)
    kv = jnp.concatenate([k, v], axis=-1)

    def to_blocks(t):
        return t.reshape(B, n_blk, Q_BLOCK, *t.shape[2:]).swapaxes(0, 1)

    def attend_block(args):
        qb, qib, wb, start = args
        q_chunk = (start + jnp.arange(Q_BLOCK)) // CHUNK
        admissible = key_chunk[None, :] <= q_chunk[:, None]
        logits = jnp.einsum('bqhd,bsd->bqhs', qib.astype(jnp.float32), ki_f) * (IDX_DIM ** -0.5)
        iscore = jnp.einsum('bqhs,bqh->bqs', jax.nn.relu(logits), wb.astype(jnp.float32))
        iscore = jnp.where(admissible[None], iscore, NEG_INF)
        _, idx = lax.top_k(iscore, topk)
        valid = (idx // CHUNK) <= q_chunk[None, :, None]
        kv_sel = jax.vmap(lambda kvb, ib: kvb[ib])(kv, idx)
        k_sel, v_sel = jnp.split(kv_sel, 2, axis=-1)
        s = jnp.einsum('bqhd,bqkhd->bhqk', qb.astype(jnp.float32),
                       k_sel.astype(jnp.float32)) * (HEAD_DIM ** -0.5)
        s = jnp.where(valid[:, None], s, NEG_INF)
        p = jax.nn.softmax(s, axis=-1).astype(dt)
        return jnp.einsum('bhqk,bqkhd->bqhd', p, v_sel)

    starts = jnp.arange(n_blk, dtype=jnp.int32) * Q_BLOCK
    o = lax.map(attend_block, (to_blocks(q), to_blocks(qi), to_blocks(wi), starts))
    return o.swapaxes(0, 1).reshape(B, L, N_HEADS * HEAD_DIM)


def setup_inputs(seed: int = 0) -> dict:
    key = jax.random.key(seed)
    ks = jax.random.split(key, 20)
    f32 = jnp.float32

    def nrm(k, shape, scale):
        return jax.random.normal(k, shape, f32) * scale

    def gain(k, n):
        return 1.0 + 0.05 * jax.random.normal(k, (DEPTH, n), f32)

    return {
        "x": jax.random.normal(ks[0], (BATCH, SEQ, D_MODEL), f32),
        "w_in": nrm(ks[1], (DEPTH, D_MODEL, D_IN), D_MODEL ** -0.5),
        "dw_w": nrm(ks[2], (DEPTH, CONV_WIDTH, D_CONV), CONV_WIDTH ** -0.5),
        "dw_b": nrm(ks[3], (DEPTH, D_CONV), 0.02),
        "conv_ln_g": gain(ks[4], D_CONV),
        "conv_ln_b": nrm(ks[5], (DEPTH, D_CONV), 0.02),
        "w_conv_out": nrm(ks[6], (DEPTH, D_CONV, D_MODEL), D_CONV ** -0.5),
        "idx_k_ln_g": gain(ks[7], IDX_DIM),
        "idx_k_ln_b": nrm(ks[8], (DEPTH, IDX_DIM), 0.02),
        "w_attn_out": nrm(ks[9], (DEPTH, D_ATTN, D_MODEL), D_ATTN ** -0.5),
        "gate_b": nrm(ks[10], (DEPTH, N_BRANCHES, D_MODEL), 0.02),
        "w_out": nrm(ks[11], (DEPTH, D_MODEL, D_MODEL), DEEPNORM_BETA * D_MODEL ** -0.5),
        "ln1_g": gain(ks[12], D_MODEL),
        "ln1_b": nrm(ks[13], (DEPTH, D_MODEL), 0.02),
        "w_ff_in": nrm(ks[14], (DEPTH, D_MODEL, D_FF), D_MODEL ** -0.5),
        "w_ff_out": nrm(ks[15], (DEPTH, D_FF, D_MODEL), DEEPNORM_BETA * D_FF ** -0.5),
        "ln2_g": gain(ks[16], D_MODEL),
        "ln2_b": nrm(ks[17], (DEPTH, D_MODEL), 0.02),
    }


def reference(x, w_in, dw_w, dw_b, conv_ln_g, conv_ln_b, w_conv_out, idx_k_ln_g, idx_k_ln_b,
              w_attn_out, gate_b, w_out, ln1_g, ln1_b, w_ff_in, w_ff_out, ln2_g, ln2_b):
    B, L, _ = x.shape
    pos = jnp.arange(L, dtype=jnp.int32)
    split_points = np.cumsum(np.array(IN_SPLIT_SIZES))[:-1].tolist()
    for layer in range(DEPTH):
        proj = x @ w_in[layer]
        (conv_a, conv_b, q, k, v, qi, ki, wi, g_conv, g_attn) = jnp.split(proj, split_points, axis=-1)

        conv_out = conformer_conv_branch(conv_a, conv_b, dw_w[layer], dw_b[layer],
                                         conv_ln_g[layer], conv_ln_b[layer], w_conv_out[layer])

        q = rope(q.reshape(B, L, N_HEADS, HEAD_DIM), pos)
        k = rope(k.reshape(B, L, N_HEADS, HEAD_DIM), pos)
        v = v.reshape(B, L, N_HEADS, HEAD_DIM)
        qi = rope(qi.reshape(B, L, N_IDX_HEADS, IDX_DIM), pos)
        ki = rope(layer_norm(ki, idx_k_ln_g[layer], idx_k_ln_b[layer])[:, :, None, :], pos)[:, :, 0, :]
        wi = wi * (N_IDX_HEADS ** -0.5)
        attn = dsa_sparse_attention(q, k, v, qi, ki, wi, pos)
        attn_out = attn @ w_attn_out[layer]

        merged = (jax.nn.sigmoid(g_conv + gate_b[layer, 0]) * conv_out
                  + jax.nn.sigmoid(g_attn + gate_b[layer, 1]) * attn_out)
        mixer = merged @ w_out[layer]
        x = layer_norm(DEEPNORM_ALPHA * x + mixer, ln1_g[layer], ln1_b[layer])

        h = jnp.square(jax.nn.relu(x @ w_ff_in[layer]))
        x = layer_norm(DEEPNORM_ALPHA * x + h @ w_ff_out[layer], ln2_g[layer], ln2_b[layer])
    return x
```

```python
import functools

import jax
import jax.numpy as jnp
from jax import lax
from jax.experimental import pallas as pl
from jax.experimental.pallas import tpu as pltpu

D_MODEL = 1024
CHUNK = 64
D_CONV = 512
CONV_WIDTH = 31
N_HEADS = 8
HEAD_DIM = 64
D_ATTN = N_HEADS * HEAD_DIM
N_IDX_HEADS = 8
IDX_DIM = 64
TOPK_MAX = 256
D_FF = 4 * D_MODEL
ROPE_THETA = 10000.0
LN_EPS = 1e-5
NEG_INF = -1e30
DEPTH = 1
DEEPNORM_ALPHA = (2.0 * DEPTH) ** 0.25

LANES = 128
SUBLANES = 8
CONV_HALO = 32
ONES_ROWS = 16
VT_ROWS = HEAD_DIM + ONES_ROWS
ACC_ROWS = 4 * SUBLANES
LOG2_E = 1.4426950408889634
INT_MIN = -(2 ** 31)
KEY_NEG_FLT_MAX = -(2 ** 31) + 0x00800000
M_FLOOR = -1e20
VMEM_LIMIT = 56 * 1024 * 1024

F32 = jnp.float32
BF16 = jnp.bfloat16


def _dot(a, b):
    return jnp.dot(a, b, preferred_element_type=F32)


def _layer_norm(x, g, b):
    mu = jnp.mean(x, axis=-1, keepdims=True)
    d = x - mu
    var = jnp.mean(d * d, axis=-1, keepdims=True)
    return d * lax.rsqrt(var + LN_EPS) * g + b


def _rope(p, cos, sin_signed):
    width = p.shape[-1]
    lane = lax.broadcasted_iota(jnp.int32, p.shape, 1)
    first_half = (lane % HEAD_DIM) < (HEAD_DIM // 2)
    swapped = jnp.where(first_half,
                        pltpu.roll(p, width - HEAD_DIM // 2, 1),
                        pltpu.roll(p, HEAD_DIM // 2, 1))
    return p * cos + swapped * sin_signed


def _inproj_kernel(x_ref, w_ref, cos_ref, sin_ref, kig_ref, kib_ref,
                   qt_ref, k_ref, vt_ref, qit_ref, ki_ref, wit_ref):
    xb = x_ref[...].astype(BF16)
    tm = xb.shape[0]
    cos = cos_ref[...]
    sin = sin_ref[...]
    q = _rope(_dot(xb, w_ref[:, 0:D_ATTN]), cos, sin) * (HEAD_DIM ** -0.5 * LOG2_E)
    qt_ref[...] = q.T.astype(BF16)
    k_ref[...] = _rope(_dot(xb, w_ref[:, D_ATTN:2 * D_ATTN]), cos, sin).astype(BF16)

    vt = _dot(xb, w_ref[:, 2 * D_ATTN:3 * D_ATTN]).T
    ones = jnp.ones((ONES_ROWS, tm), F32)
    pieces = []
    for h in range(N_HEADS):
        pieces += [vt[h * HEAD_DIM:(h + 1) * HEAD_DIM], ones]
    vt_ref[...] = jnp.concatenate(pieces, axis=0).astype(BF16)

    qi = _rope(_dot(xb, w_ref[:, 3 * D_ATTN:4 * D_ATTN]), cos, sin) * (IDX_DIM ** -0.5)
    qit_ref[...] = qi.T.astype(BF16)

    sm = _dot(xb, w_ref[:, 4 * D_ATTN:4 * D_ATTN + LANES])
    lane = lax.broadcasted_iota(jnp.int32, sm.shape, 1)
    is_ki = lane < IDX_DIM
    mu = jnp.sum(jnp.where(is_ki, sm, 0.0), axis=-1, keepdims=True) * (1.0 / IDX_DIM)
    d = jnp.where(is_ki, sm - mu, 0.0)
    var = jnp.sum(d * d, axis=-1, keepdims=True) * (1.0 / IDX_DIM)
    kn = d * lax.rsqrt(var + LN_EPS) * kig_ref[...] + kib_ref[...]
    kr = _rope(kn, cos[:, :LANES], sin[:, :LANES])
    ki_ref[...] = kr[:, :IDX_DIM].astype(BF16)
    wit_ref[...] = sm.T[IDX_DIM:IDX_DIM + N_IDX_HEADS] * (N_IDX_HEADS ** -0.5)


def _inproj(x2, w_attn, cos_t, sin_t, kig, kib, *, batch, seq, tm):
    n = batch * seq
    nt = seq // tm
    row = lambda j, b: (b * nt + j, 0)
    col = lambda j, b: (b, j)
    pos = lambda j, b: (j, 0)
    const = lambda j, b: (0, 0)
    wcols = w_attn.shape[1]
    return pl.pallas_call(
        _inproj_kernel,
        grid=(nt, batch),
        in_specs=[
            pl.BlockSpec((tm, D_MODEL), row),
            pl.BlockSpec((D_MODEL, wcols), const),
            pl.BlockSpec((tm, D_ATTN), pos),
            pl.BlockSpec((tm, D_ATTN), pos),
            pl.BlockSpec((1, LANES), const),
            pl.BlockSpec((1, LANES), const),
        ],
        out_specs=[
            pl.BlockSpec((D_ATTN, tm), col),
            pl.BlockSpec((tm, D_ATTN), row),
            pl.BlockSpec((None, None, N_HEADS * VT_ROWS, tm), lambda j, b: (b, j, 0, 0)),
            pl.BlockSpec((D_ATTN, tm), col),
            pl.BlockSpec((tm, IDX_DIM), row),
            pl.BlockSpec((N_IDX_HEADS, tm), col),
        ],
        out_shape=[
            jax.ShapeDtypeStruct((batch * D_ATTN, seq), BF16),
            jax.ShapeDtypeStruct((n, D_ATTN), BF16),
            jax.ShapeDtypeStruct((batch, nt, N_HEADS * VT_ROWS, tm), BF16),
            jax.ShapeDtypeStruct((batch * D_ATTN, seq), BF16),
            jax.ShapeDtypeStruct((n, IDX_DIM), BF16),
            jax.ShapeDtypeStruct((batch * N_IDX_HEADS, seq), F32),
        ],
        compiler_params=pltpu.CompilerParams(
            dimension_semantics=("arbitrary", "arbitrary"), vmem_limit_bytes=VMEM_LIMIT),
        name="inproj",
    )(x2, w_attn, cos_t, sin_t, kig, kib)


def _key_to_f32(key):
    bits = key ^ ((key >> 31) & jnp.int32(0x7FFFFFFF))
    return lax.bitcast_convert_type(bits, F32)


def _fold_rows(x, op):
    rows, cols = x.shape
    return op(x.reshape(rows // ACC_ROWS, ACC_ROWS, cols), axis=0)


def _dsa_kernel(qit_ref, wit_ref, ki_ref, qt_ref, k_ref, vt_ref, o_ref,
                isc_s, qz_s, thr_s, m_s, acc_s, s_st, p_st, *, tq, tk, topk, seq):
    j = pl.program_id(1)
    t0 = j * tq
    n_kb = (t0 + tq + tk - 1) // tk

    chunk_shift = CHUNK.bit_length() - 1
    q_chunk = (t0 + lax.broadcasted_iota(jnp.int32, (tk, tq), 1)) >> chunk_shift
    key_off = lax.broadcasted_iota(jnp.int32, (tk, tq), 0)

    def score_block(kb, carry):
        kib = ki_ref[pl.ds(pl.multiple_of(kb * tk, tk), tk), :]
        isc = jnp.zeros((tk, tq), F32)
        for h in range(N_IDX_HEADS):
            lg = _dot(kib, qit_ref[h * IDX_DIM:(h + 1) * IDX_DIM, :])
            isc = isc + jnp.maximum(lg, 0.0) * wit_ref[h:h + 1, :]
        admissible = ((kb * tk + key_off) >> chunk_shift) <= q_chunk
        isc_s[kb] = jnp.where(admissible, isc, -jnp.inf)
        return carry

    lax.fori_loop(0, n_kb, score_block, 0)

    def count_keys(pred_fn):
        def body(kb, part):
            hit = jnp.where(pred_fn(kb, isc_s[kb]), 1, 0)
            return part + _fold_rows(hit, jnp.sum)
        part = lax.fori_loop(0, n_kb, body, jnp.zeros((ACC_ROWS, tq), jnp.int32))
        return jnp.sum(part, axis=0, keepdims=True)

    def bit_step(i, thr):
        cand = jnp.where(i == 0, jnp.zeros_like(thr), thr | (jnp.int32(1) << (31 - i)))
        cand_f = _key_to_f32(jnp.maximum(cand, KEY_NEG_FLT_MAX))
        cnt = count_keys(lambda kb, blk: blk >= cand_f)
        return jnp.where(cnt >= topk, cand, thr)

    thr_key = lax.fori_loop(0, 32, bit_step, jnp.full((1, tq), INT_MIN, jnp.int32))
    thr = _key_to_f32(jnp.maximum(thr_key, KEY_NEG_FLT_MAX))
    thr_s[...] = jnp.broadcast_to(thr, thr_s.shape)

    n_ge = count_keys(lambda kb, blk: blk >= thr)
    has_ties = jnp.max(n_ge) > topk

    def to_bias(select_fn):
        def body(kb, carry):
            isc_s[kb] = jnp.where(select_fn(kb, isc_s[kb]), 0.0, NEG_INF)
            return carry
        lax.fori_loop(0, n_kb, body, 0)

    @pl.when(jnp.logical_not(has_ties))
    def _():
        thr_v = thr_s[0:1, :]
        to_bias(lambda kb, blk: blk >= thr_v)

    @pl.when(has_ties)
    def _():
        thr_v = thr_s[0:1, :]
        n_gt = count_keys(lambda kb, blk: blk > thr_v)
        need = topk - n_gt
        n_bits = seq.bit_length()

        def idx_step(i, lim):
            cand = lim | (jnp.int32(1) << (n_bits - 1 - i))
            cnt = count_keys(lambda kb, blk: (blk == thr_v) & (kb * tk + key_off < cand))
            return jnp.where(cnt < need, cand, lim)

        lim = lax.fori_loop(0, n_bits, idx_step, jnp.zeros((1, tq), jnp.int32))
        last_tie = jnp.where(n_ge > topk, lim, seq)
        to_bias(lambda kb, blk: (blk > thr_v) | ((blk == thr_v) & (kb * tk + key_off <= last_tie)))

    zeros_half = jnp.zeros((HEAD_DIM, tq), BF16)
    for h in range(N_HEADS):
        head_rows = qt_ref[h * HEAD_DIM:(h + 1) * HEAD_DIM, :]
        pair = [head_rows, zeros_half] if h % 2 == 0 else [zeros_half, head_rows]
        qz_s[h] = jnp.concatenate(pair, axis=0)
    m_s[...] = jnp.full(m_s.shape, M_FLOOR, F32)
    acc_s[...] = jnp.zeros(acc_s.shape, F32)

    def attend_block(kb, carry):
        rows = pl.ds(pl.multiple_of(kb * tk, tk), tk)
        m_blk = []
        for h in range(N_HEADS):
            slab = slice((h // 2) * LANES, (h // 2 + 1) * LANES)
            s = _dot(k_ref[rows, slab], qz_s[h]) + isc_s[kb]
            s_st[h] = s
            m_blk.append(jnp.max(_fold_rows(s, jnp.max), axis=0, keepdims=True))
        alpha = []
        for h in range(N_HEADS):
            m_old = m_s[h]
            m_new = jnp.maximum(m_old, m_blk[h])
            alpha.append(jnp.exp2(m_old - m_new)[0:1, :])
            p_st[h] = jnp.exp2(s_st[h] - m_new[0:1, :]).astype(BF16)
            m_s[h] = m_new
        for h in range(N_HEADS):
            pv = _dot(vt_ref[kb, h * VT_ROWS:(h + 1) * VT_ROWS, :], p_st[h])
            acc_s[h] = alpha[h] * acc_s[h] + pv
        return carry

    lax.fori_loop(0, n_kb, attend_block, 0)

    outs = []
    for h in range(N_HEADS):
        a = acc_s[h]
        outs.append(a[:HEAD_DIM] / a[HEAD_DIM:HEAD_DIM + 1])
    o_ref[...] = jnp.concatenate(outs, axis=0).T.astype(BF16)


def _dsa(qit, wit, ki, qt, k, vt, *, batch, seq, tq, tk):
    n = batch * seq
    nt = seq // tq
    topk = min(TOPK_MAX, seq // 4)
    n_blk = seq // tk
    qcol = lambda b, j: (b, j)
    whole = lambda b, j: (b, 0)
    kern = functools.partial(_dsa_kernel, tq=tq, tk=tk, topk=topk, seq=seq)
    return pl.pallas_call(
        kern,
        grid=(batch, nt),
        in_specs=[
            pl.BlockSpec((D_ATTN, tq), qcol),
            pl.BlockSpec((N_IDX_HEADS, tq), qcol),
            pl.BlockSpec((seq, IDX_DIM), whole),
            pl.BlockSpec((D_ATTN, tq), qcol),
            pl.BlockSpec((seq, D_ATTN), whole),
            pl.BlockSpec((None, n_blk, N_HEADS * VT_ROWS, tk), lambda b, j: (b, 0, 0, 0)),
        ],
        out_specs=pl.BlockSpec((tq, D_ATTN), lambda b, j: (b * nt + j, 0)),
        out_shape=jax.ShapeDtypeStruct((n, D_ATTN), BF16),
        scratch_shapes=[
            pltpu.VMEM((n_blk, tk, tq), F32),
            pltpu.VMEM((N_HEADS, LANES, tq), BF16),
            pltpu.VMEM((SUBLANES, tq), F32),
            pltpu.VMEM((N_HEADS, SUBLANES, tq), F32),
            pltpu.VMEM((N_HEADS, VT_ROWS, tq), F32),
            pltpu.VMEM((N_HEADS, tk, tq), F32),
            pltpu.VMEM((N_HEADS, tk, tq), BF16),
        ],
        compiler_params=pltpu.CompilerParams(
            dimension_semantics=("arbitrary", "arbitrary"), vmem_limit_bytes=VMEM_LIMIT),
        name="dsa",
    )(qit, wit, ki, qt, k, vt)


def _conv_kernel(x_ref, xh_ref, wab_ref, dww_ref, dwb_ref, g_ref, b_ref, o_ref, u_s, c_s, *, tm):
    j = pl.program_id(1)

    def glu(xb):
        ab = _dot(xb, wab_ref[...])
        return ab[:, :D_CONV] * jax.nn.sigmoid(ab[:, D_CONV:])

    u_halo = glu(xh_ref[...].astype(BF16))
    u_s[0:CONV_HALO, :] = jnp.where(j > 0, u_halo, 0.0)
    u_s[CONV_HALO:CONV_HALO + tm, :] = glu(x_ref[...].astype(BF16))

    rows_per_step = 64
    first = CONV_HALO - (CONV_WIDTH - 1)
    for r0 in range(0, tm, rows_per_step):
        for c0 in range(0, D_CONV, LANES):
            acc = jnp.broadcast_to(dwb_ref[:, c0:c0 + LANES], (rows_per_step, LANES))
            for tap in range(CONV_WIDTH):
                acc = acc + (u_s[r0 + first + tap:r0 + first + tap + rows_per_step, c0:c0 + LANES]
                             * dww_ref[tap:tap + 1, c0:c0 + LANES])
            c_s[r0:r0 + rows_per_step, c0:c0 + LANES] = acc

    y = _layer_norm(c_s[...], g_ref[...], b_ref[...])
    o_ref[...] = (y * jax.nn.sigmoid(y)).astype(BF16)


def _conv_branch(x2, w_ab, dw_w, dw_b, ln_g, ln_b, *, batch, seq, tm):
    n = batch * seq
    nt = seq // tm
    halo_per_tile = tm // CONV_HALO
    halo_per_seq = seq // CONV_HALO
    row = lambda b, j: (b * nt + j, 0)
    halo = lambda b, j: (jnp.maximum(b * halo_per_seq + j * halo_per_tile - 1, 0), 0)
    const = lambda b, j: (0, 0)
    return pl.pallas_call(
        functools.partial(_conv_kernel, tm=tm),
        grid=(batch, nt),
        in_specs=[
            pl.BlockSpec((tm, D_MODEL), row),
            pl.BlockSpec((CONV_HALO, D_MODEL), halo),
            pl.BlockSpec((D_MODEL, 2 * D_CONV), const),
            pl.BlockSpec((CONV_HALO, D_CONV), const),
            pl.BlockSpec((1, D_CONV), const),
            pl.BlockSpec((1, D_CONV), const),
            pl.BlockSpec((1, D_CONV), const),
        ],
        out_specs=pl.BlockSpec((tm, D_CONV), row),
        out_shape=jax.ShapeDtypeStruct((n, D_CONV), BF16),
        scratch_shapes=[
            pltpu.VMEM((CONV_HALO + tm, D_CONV), F32),
            pltpu.VMEM((tm, D_CONV), F32),
        ],
        compiler_params=pltpu.CompilerParams(
            dimension_semantics=("arbitrary", "arbitrary"), vmem_limit_bytes=VMEM_LIMIT),
        name="conv_branch",
    )(x2, x2, w_ab, dw_w, dw_b, ln_g, ln_b)


def _mixer_kernel(x_ref, cf_ref, at_ref, wg_ref, gb_ref, wco_ref, wao_ref, wout_ref, g_ref, b_ref, o_ref):
    x = x_ref[...]
    xb = x.astype(BF16)
    gate_c = jax.nn.sigmoid(_dot(xb, wg_ref[:, :D_MODEL]) + gb_ref[:, :D_MODEL])
    merged = gate_c * _dot(cf_ref[...], wco_ref[...])
    gate_a = jax.nn.sigmoid(_dot(xb, wg_ref[:, D_MODEL:]) + gb_ref[:, D_MODEL:])
    merged = merged + gate_a * _dot(at_ref[...], wao_ref[...])
    mixer = _dot(merged.astype(BF16), wout_ref[...])
    o_ref[...] = _layer_norm(DEEPNORM_ALPHA * x + mixer, g_ref[...], b_ref[...])


def _mixer(x2, cf, at, w_g, gate_b, w_co, w_ao, w_out, ln_g, ln_b, *, tm):
    n = x2.shape[0]
    row = lambda i: (i, 0)
    const = lambda i: (0, 0)
    return pl.pallas_call(
        _mixer_kernel,
        grid=(n // tm,),
        in_specs=[
            pl.BlockSpec((tm, D_MODEL), row),
            pl.BlockSpec((tm, D_CONV), row),
            pl.BlockSpec((tm, D_ATTN), row),
            pl.BlockSpec((D_MODEL, 2 * D_MODEL), const),
            pl.BlockSpec((1, 2 * D_MODEL), const),
            pl.BlockSpec((D_CONV, D_MODEL), const),
            pl.BlockSpec((D_ATTN, D_MODEL), const),
            pl.BlockSpec((D_MODEL, D_MODEL), const),
            pl.BlockSpec((1, D_MODEL), const),
            pl.BlockSpec((1, D_MODEL), const),
        ],
        out_specs=pl.BlockSpec((tm, D_MODEL), row),
        out_shape=jax.ShapeDtypeStruct((n, D_MODEL), F32),
        compiler_params=pltpu.CompilerParams(
            dimension_semantics=("arbitrary",), vmem_limit_bytes=VMEM_LIMIT),
        name="mixer",
    )(x2, cf, at, w_g, gate_b, w_co, w_ao, w_out, ln_g, ln_b)


def _ffn_kernel(x_ref, wi_ref, wo_ref, g_ref, b_ref, o_ref, *, ff_chunk):
    x = x_ref[...]
    xb = x.astype(BF16)
    acc = jnp.zeros(x.shape, F32)
    for c0 in range(0, D_FF, ff_chunk):
        h = jnp.maximum(_dot(xb, wi_ref[:, c0:c0 + ff_chunk]), 0.0)
        acc = acc + _dot((h * h).astype(BF16), wo_ref[c0:c0 + ff_chunk, :])
    o_ref[...] = _layer_norm(DEEPNORM_ALPHA * x + acc, g_ref[...], b_ref[...])


def _ffn(x2, w_i, w_o, ln_g, ln_b, *, tm, ff_chunk):
    n = x2.shape[0]
    row = lambda i: (i, 0)
    const = lambda i: (0, 0)
    return pl.pallas_call(
        functools.partial(_ffn_kernel, ff_chunk=ff_chunk),
        grid=(n // tm,),
        in_specs=[
            pl.BlockSpec((tm, D_MODEL), row),
            pl.BlockSpec((D_MODEL, D_FF), const, pipeline_mode=pl.Buffered(1)),
            pl.BlockSpec((D_FF, D_MODEL), const, pipeline_mode=pl.Buffered(1)),
            pl.BlockSpec((1, D_MODEL), const),
            pl.BlockSpec((1, D_MODEL), const),
        ],
        out_specs=pl.BlockSpec((tm, D_MODEL), row),
        out_shape=jax.ShapeDtypeStruct((n, D_MODEL), F32),
        compiler_params=pltpu.CompilerParams(
            dimension_semantics=("arbitrary",), vmem_limit_bytes=VMEM_LIMIT),
        name="ffn",
    )(x2, w_i, w_o, ln_g, ln_b)


def _rope_tables(seq):
    inv_freq = ROPE_THETA ** (-jnp.arange(0, HEAD_DIM, 2, dtype=F32) / HEAD_DIM)
    ang = jnp.arange(seq, dtype=jnp.int32).astype(F32)[:, None] * inv_freq[None, :]
    cos = jnp.cos(ang)
    sin = jnp.sin(ang)
    cos_blk = jnp.concatenate([cos, cos], axis=1)
    sin_blk = jnp.concatenate([-sin, sin], axis=1)
    return jnp.tile(cos_blk, (1, N_HEADS)), jnp.tile(sin_blk, (1, N_HEADS))


def _pad_lanes(v, width):
    return jnp.pad(v, (0, width - v.shape[0])).reshape(1, width)


def kernel(x, w_in, dw_w, dw_b, conv_ln_g, conv_ln_b, w_conv_out, idx_k_ln_g, idx_k_ln_b,
           w_attn_out, gate_b, w_out, ln1_g, ln1_b, w_ff_in, w_ff_out, ln2_g, ln2_b):
    batch, seq, _ = x.shape
    cos_t, sin_t = _rope_tables(seq)
    tm = min(512, seq)
    tq = min(256, seq)
    tk = min(512, seq)

    o_a, o_b, o_q, o_k, o_v = 0, D_CONV, 2 * D_CONV, 2 * D_CONV + D_ATTN, 2 * D_CONV + 2 * D_ATTN
    o_qi = o_v + D_ATTN
    o_ki = o_qi + N_IDX_HEADS * IDX_DIM
    o_wi = o_ki + IDX_DIM
    o_gc = o_wi + N_IDX_HEADS
    o_ga = o_gc + D_MODEL

    h = x.reshape(batch * seq, D_MODEL)
    for layer in range(w_in.shape[0]):
        w = w_in[layer]
        small = jnp.pad(w[:, o_ki:o_gc], ((0, 0), (0, LANES - (o_gc - o_ki))))
        w_attn = jnp.concatenate([w[:, o_q:o_ki], small], axis=1).astype(BF16)
        w_ab = w[:, o_a:o_q].astype(BF16)
        w_g = w[:, o_gc:o_ga + D_MODEL].astype(BF16)

        qt, k, vt, qit, ki, wit = _inproj(
            h, w_attn, cos_t, sin_t,
            _pad_lanes(idx_k_ln_g[layer], LANES), _pad_lanes(idx_k_ln_b[layer], LANES),
            batch=batch, seq=seq, tm=tk)
        attn = _dsa(qit, wit, ki, qt, k, vt, batch=batch, seq=seq, tq=tq, tk=tk)

        dww = jnp.pad(dw_w[layer], ((0, CONV_HALO - CONV_WIDTH), (0, 0)))
        conv_feat = _conv_branch(
            h, w_ab, dww, dw_b[layer].reshape(1, D_CONV),
            conv_ln_g[layer].reshape(1, D_CONV), conv_ln_b[layer].reshape(1, D_CONV),
            batch=batch, seq=seq, tm=tm)

        h = _mixer(h, conv_feat, attn, w_g, gate_b[layer].reshape(1, 2 * D_MODEL),
                   w_conv_out[layer].astype(BF16), w_attn_out[layer].astype(BF16),
                   w_out[layer].astype(BF16),
                   ln1_g[layer].reshape(1, D_MODEL), ln1_b[layer].reshape(1, D_MODEL), tm=tm)
        h = _ffn(h, w_ff_in[layer].astype(BF16), w_ff_out[layer].astype(BF16),
                 ln2_g[layer].reshape(1, D_MODEL), ln2_b[layer].reshape(1, D_MODEL),
                 tm=tm, ff_chunk=1024)
    return h.reshape(batch, seq, D_MODEL)
```

```python
import functools

import jax
import jax.numpy as jnp
from jax import lax
from jax.experimental import pallas as pl
from jax.experimental.pallas import tpu as pltpu

D_MODEL = 1024
CHUNK = 64
D_CONV = 512
CONV_WIDTH = 31
N_HEADS = 8
HEAD_DIM = 64
D_ATTN = N_HEADS * HEAD_DIM
N_IDX_HEADS = 8
IDX_DIM = 64
TOPK_MAX = 256
D_FF = 4 * D_MODEL
ROPE_THETA = 10000.0
LN_EPS = 1e-5
NEG_INF = -1e30
DEPTH = 1
DEEPNORM_ALPHA = (2.0 * DEPTH) ** 0.25

LANES = 128
SUBLANES = 8
CONV_HALO = 32
ONES_ROWS = 16
VT_ROWS = HEAD_DIM + ONES_ROWS
ACC_ROWS = 4 * SUBLANES
SOFTMAX_ROWS = 128
LOG2_E = 1.4426950408889634
INT_MIN = -(2 ** 31)
KEY_NEG_FLT_MAX = -(2 ** 31) + 0x00800000
M_FLOOR = -1e20
VMEM_LIMIT = 56 * 1024 * 1024

F32 = jnp.float32
BF16 = jnp.bfloat16


def _dot(a, b):
    return jnp.dot(a, b, preferred_element_type=F32)


def _layer_norm(x, g, b):
    mu = jnp.mean(x, axis=-1, keepdims=True)
    d = x - mu
    var = jnp.mean(d * d, axis=-1, keepdims=True)
    return d * lax.rsqrt(var + LN_EPS) * g + b


def _rope(p, cos, sin_signed):
    width = p.shape[-1]
    lane = lax.broadcasted_iota(jnp.int32, p.shape, 1)
    first_half = (lane % HEAD_DIM) < (HEAD_DIM // 2)
    swapped = jnp.where(first_half,
                        pltpu.roll(p, width - HEAD_DIM // 2, 1),
                        pltpu.roll(p, HEAD_DIM // 2, 1))
    return p * cos + swapped * sin_signed


def _inproj_kernel(x_ref, w_ref, cos_ref, sin_ref, kig_ref, kib_ref,
                   qt_ref, k_ref, vt_ref, qit_ref, ki_ref, wit_ref):
    xb = x_ref[...].astype(BF16)
    tm = xb.shape[0]
    cos = cos_ref[...]
    sin = sin_ref[...]
    q = _rope(_dot(xb, w_ref[:, 0:D_ATTN]), cos, sin) * (HEAD_DIM ** -0.5 * LOG2_E)
    qt_ref[...] = q.T.astype(BF16)
    k_ref[...] = _rope(_dot(xb, w_ref[:, D_ATTN:2 * D_ATTN]), cos, sin).astype(BF16)

    vt = _dot(xb, w_ref[:, 2 * D_ATTN:3 * D_ATTN]).T
    ones = jnp.ones((ONES_ROWS, tm), F32)
    pieces = []
    for h in range(N_HEADS):
        pieces += [vt[h * HEAD_DIM:(h + 1) * HEAD_DIM], ones]
    vt_ref[...] = jnp.concatenate(pieces, axis=0).astype(BF16)

    qi = _rope(_dot(xb, w_ref[:, 3 * D_ATTN:4 * D_ATTN]), cos, sin) * (IDX_DIM ** -0.5)
    qit_ref[...] = qi.T.astype(BF16)

    sm = _dot(xb, w_ref[:, 4 * D_ATTN:4 * D_ATTN + LANES])
    lane = lax.broadcasted_iota(jnp.int32, sm.shape, 1)
    is_ki = lane < IDX_DIM
    mu = jnp.sum(jnp.where(is_ki, sm, 0.0), axis=-1, keepdims=True) * (1.0 / IDX_DIM)
    d = jnp.where(is_ki, sm - mu, 0.0)
    var = jnp.sum(d * d, axis=-1, keepdims=True) * (1.0 / IDX_DIM)
    kn = d * lax.rsqrt(var + LN_EPS) * kig_ref[...] + kib_ref[...]
    kr = _rope(kn, cos[:, :LANES], sin[:, :LANES])
    ki_ref[...] = kr[:, :IDX_DIM].astype(BF16)
    wit_ref[...] = sm.T[IDX_DIM:IDX_DIM + N_IDX_HEADS] * (N_IDX_HEADS ** -0.5)


def _inproj(x2, w_attn, cos_t, sin_t, kig, kib, *, batch, seq, tm):
    n = batch * seq
    nt = seq // tm
    row = lambda j, b: (b * nt + j, 0)
    col = lambda j, b: (b, j)
    pos = lambda j, b: (j, 0)
    const = lambda j, b: (0, 0)
    wcols = w_attn.shape[1]
    return pl.pallas_call(
        _inproj_kernel,
        grid=(nt, batch),
        in_specs=[
            pl.BlockSpec((tm, D_MODEL), row),
            pl.BlockSpec((D_MODEL, wcols), const),
            pl.BlockSpec((tm, D_ATTN), pos),
            pl.BlockSpec((tm, D_ATTN), pos),
            pl.BlockSpec((1, LANES), const),
            pl.BlockSpec((1, LANES), const),
        ],
        out_specs=[
            pl.BlockSpec((D_ATTN, tm), col),
            pl.BlockSpec((tm, D_ATTN), row),
            pl.BlockSpec((None, None, N_HEADS * VT_ROWS, tm), lambda j, b: (b, j, 0, 0)),
            pl.BlockSpec((D_ATTN, tm), col),
            pl.BlockSpec((tm, IDX_DIM), row),
            pl.BlockSpec((N_IDX_HEADS, tm), col),
        ],
        out_shape=[
            jax.ShapeDtypeStruct((batch * D_ATTN, seq), BF16),
            jax.ShapeDtypeStruct((n, D_ATTN), BF16),
            jax.ShapeDtypeStruct((batch, nt, N_HEADS * VT_ROWS, tm), BF16),
            jax.ShapeDtypeStruct((batch * D_ATTN, seq), BF16),
            jax.ShapeDtypeStruct((n, IDX_DIM), BF16),
            jax.ShapeDtypeStruct((batch * N_IDX_HEADS, seq), F32),
        ],
        compiler_params=pltpu.CompilerParams(
            dimension_semantics=("arbitrary", "arbitrary"), vmem_limit_bytes=VMEM_LIMIT),
        name="inproj",
    )(x2, w_attn, cos_t, sin_t, kig, kib)


def _key_to_f32(key):
    bits = key ^ ((key >> 31) & jnp.int32(0x7FFFFFFF))
    return lax.bitcast_convert_type(bits, F32)


def _fold_rows(x, op):
    rows, cols = x.shape
    return op(x.reshape(rows // ACC_ROWS, ACC_ROWS, cols), axis=0)


def _dsa_kernel(qit_ref, wit_ref, ki_ref, qt_ref, k_ref, vt_ref, o_ref,
                isc_s, qz_s, thr_s, m_s, mblk_s, alpha_s, acc_s, s_st, p_st, *, tq, tk, topk, seq):
    j = pl.program_id(1)
    t0 = j * tq
    n_kb = (t0 + tq + tk - 1) // tk

    chunk_shift = CHUNK.bit_length() - 1
    q_chunk = (t0 + lax.broadcasted_iota(jnp.int32, (tk, tq), 1)) >> chunk_shift
    key_off = lax.broadcasted_iota(jnp.int32, (tk, tq), 0)

    def score_block(kb, carry):
        kib = ki_ref[pl.ds(pl.multiple_of(kb * tk, tk), tk), :]
        isc = jnp.zeros((tk, tq), F32)
        for h in range(N_IDX_HEADS):
            lg = _dot(kib, qit_ref[h * IDX_DIM:(h + 1) * IDX_DIM, :])
            isc = isc + jnp.maximum(lg, 0.0) * wit_ref[h:h + 1, :]
        admissible = ((kb * tk + key_off) >> chunk_shift) <= q_chunk
        isc_s[kb] = jnp.where(admissible, isc, -jnp.inf)
        return carry

    lax.fori_loop(0, n_kb, score_block, 0)

    def count_keys(pred_fn):
        def body(kb, part):
            hit = jnp.where(pred_fn(kb, isc_s[kb]), 1, 0)
            return part + _fold_rows(hit, jnp.sum)
        part = lax.fori_loop(0, n_kb, body, jnp.zeros((ACC_ROWS, tq), jnp.int32))
        return jnp.sum(part, axis=0, keepdims=True)

    def bit_step(i, carry):
        thr, n_ge = carry
        cand = jnp.where(i == 0, jnp.zeros_like(thr), thr | (jnp.int32(1) << (31 - i)))
        cand_f = _key_to_f32(jnp.maximum(cand, KEY_NEG_FLT_MAX))
        cnt = count_keys(lambda kb, blk: blk >= cand_f)
        accept = cnt >= topk
        return jnp.where(accept, cand, thr), jnp.where(accept, cnt, n_ge)

    thr_key, n_ge = lax.fori_loop(
        0, 32, bit_step, (jnp.full((1, tq), INT_MIN, jnp.int32), jnp.zeros((1, tq), jnp.int32)))
    thr = _key_to_f32(jnp.maximum(thr_key, KEY_NEG_FLT_MAX))
    thr_s[...] = jnp.broadcast_to(thr, thr_s.shape)

    has_ties = jnp.max(n_ge) > topk

    def to_bias(select_fn):
        def body(kb, carry):
            isc_s[kb] = jnp.where(select_fn(kb, isc_s[kb]), 0.0, NEG_INF)
            return carry
        lax.fori_loop(0, n_kb, body, 0)

    @pl.when(jnp.logical_not(has_ties))
    def _():
        thr_v = thr_s[0:1, :]
        to_bias(lambda kb, blk: blk >= thr_v)

    @pl.when(has_ties)
    def _():
        thr_v = thr_s[0:1, :]
        n_gt = count_keys(lambda kb, blk: blk > thr_v)
        need = topk - n_gt
        n_bits = seq.bit_length()

        def idx_step(i, lim):
            cand = lim | (jnp.int32(1) << (n_bits - 1 - i))
            cnt = count_keys(lambda kb, blk: (blk == thr_v) & (kb * tk + key_off < cand))
            return jnp.where(cnt < need, cand, lim)

        lim = lax.fori_loop(0, n_bits, idx_step, jnp.zeros((1, tq), jnp.int32))
        last_tie = jnp.where(n_ge > topk, lim, seq)
        to_bias(lambda kb, blk: (blk > thr_v) | ((blk == thr_v) & (kb * tk + key_off <= last_tie)))

    zeros_half = jnp.zeros((HEAD_DIM, tq), BF16)
    for h in range(N_HEADS):
        head_rows = qt_ref[h * HEAD_DIM:(h + 1) * HEAD_DIM, :]
        pair = [head_rows, zeros_half] if h % 2 == 0 else [zeros_half, head_rows]
        qz_s[h] = jnp.concatenate(pair, axis=0)
    m_s[...] = jnp.full(m_s.shape, M_FLOOR, F32)
    acc_s[...] = jnp.zeros(acc_s.shape, F32)

    chunks = [slice(c, c + SOFTMAX_ROWS) for c in range(0, tk, SOFTMAX_ROWS)]

    def scores(kb, h):
        rows = pl.ds(pl.multiple_of(kb * tk, tk), tk)
        slab = slice((h // 2) * LANES, (h // 2 + 1) * LANES)
        s = _dot(k_ref[rows, slab], qz_s[h]) + isc_s[kb]
        s_st[h] = s
        mblk_s[h] = jnp.broadcast_to(
            jnp.max(_fold_rows(s, jnp.max), axis=0, keepdims=True), (SUBLANES, tq))

    def probs(h):
        m_old = m_s[h]
        m_new = jnp.maximum(m_old, mblk_s[h])
        alpha_s[h] = jnp.exp2(m_old - m_new)
        for c in chunks:
            p_st[h, c, :] = jnp.exp2(s_st[h, c, :] - m_new[0:1, :]).astype(BF16)
        m_s[h] = m_new

    def accum(kb, h):
        pv = _dot(vt_ref[kb, h * VT_ROWS:(h + 1) * VT_ROWS, :], p_st[h])
        acc_s[h] = alpha_s[h][0:1, :] * acc_s[h] + pv

    for h in range(N_HEADS):
        scores(0, h)
    for h in range(N_HEADS):
        probs(h)

    @pl.when(n_kb >= 2)
    def _():
        for h in range(N_HEADS):
            scores(1, h)

    def steady(t, carry):
        for h in range(N_HEADS):
            accum(t - 2, h)
        for h in range(N_HEADS):
            probs(h)
        for h in range(N_HEADS):
            scores(t, h)
        return carry

    lax.fori_loop(2, n_kb, steady, 0)

    @pl.when(n_kb >= 2)
    def _():
        for h in range(N_HEADS):
            accum(n_kb - 2, h)
        for h in range(N_HEADS):
            probs(h)

    for h in range(N_HEADS):
        accum(n_kb - 1, h)

    outs = []
    for h in range(N_HEADS):
        a = acc_s[h]
        outs.append(a[:HEAD_DIM] / a[HEAD_DIM:HEAD_DIM + 1])
    o_ref[...] = jnp.concatenate(outs, axis=0).T.astype(BF16)


def _dsa(qit, wit, ki, qt, k, vt, *, batch, seq, tq, tk):
    n = batch * seq
    nt = seq // tq
    topk = min(TOPK_MAX, seq // 4)
    n_blk = seq // tk
    qcol = lambda b, j: (b, j)
    whole = lambda b, j: (b, 0)
    kern = functools.partial(_dsa_kernel, tq=tq, tk=tk, topk=topk, seq=seq)
    return pl.pallas_call(
        kern,
        grid=(batch, nt),
        in_specs=[
            pl.BlockSpec((D_ATTN, tq), qcol),
            pl.BlockSpec((N_IDX_HEADS, tq), qcol),
            pl.BlockSpec((seq, IDX_DIM), whole),
            pl.BlockSpec((D_ATTN, tq), qcol),
            pl.BlockSpec((seq, D_ATTN), whole),
            pl.BlockSpec((None, n_blk, N_HEADS * VT_ROWS, tk), lambda b, j: (b, 0, 0, 0)),
        ],
        out_specs=pl.BlockSpec((tq, D_ATTN), lambda b, j: (b * nt + j, 0)),
        out_shape=jax.ShapeDtypeStruct((n, D_ATTN), BF16),
        scratch_shapes=[
            pltpu.VMEM((n_blk, tk, tq), F32),
            pltpu.VMEM((N_HEADS, LANES, tq), BF16),
            pltpu.VMEM((SUBLANES, tq), F32),
            pltpu.VMEM((N_HEADS, SUBLANES, tq), F32),
            pltpu.VMEM((N_HEADS, SUBLANES, tq), F32),
            pltpu.VMEM((N_HEADS, SUBLANES, tq), F32),
            pltpu.VMEM((N_HEADS, VT_ROWS, tq), F32),
            pltpu.VMEM((N_HEADS, tk, tq), F32),
            pltpu.VMEM((N_HEADS, tk, tq), BF16),
        ],
        compiler_params=pltpu.CompilerParams(
            dimension_semantics=("arbitrary", "arbitrary"), vmem_limit_bytes=VMEM_LIMIT),
        name="dsa",
    )(qit, wit, ki, qt, k, vt)


def _conv_kernel(x_ref, xh_ref, wab_ref, dww_ref, dwb_ref, g_ref, b_ref, o_ref, u_s, ush_s, c_s, *, tm):
    j = pl.program_id(1)

    def glu(xb):
        ab = _dot(xb, wab_ref[...])
        return ab[:, :D_CONV] * jax.nn.sigmoid(ab[:, D_CONV:])

    u_halo = glu(xh_ref[...].astype(BF16))
    u_s[0:CONV_HALO, :] = jnp.where(j > 0, u_halo, 0.0)
    u_s[CONV_HALO:CONV_HALO + tm, :] = glu(x_ref[...].astype(BF16))

    shifted_rows = CONV_HALO + tm - SUBLANES
    for r in range(1, SUBLANES):
        ush_s[r - 1, 0:shifted_rows, :] = u_s[r:r + shifted_rows, :]

    rows_per_step = 64
    first = CONV_HALO - (CONV_WIDTH - 1)
    for r0 in range(0, tm, rows_per_step):
        for c0 in range(0, D_CONV, LANES):
            acc = jnp.broadcast_to(dwb_ref[:, c0:c0 + LANES], (rows_per_step, LANES))
            for tap in range(CONV_WIDTH):
                phase = (first + tap) % SUBLANES
                base = r0 + (first + tap) - phase
                src = u_s if phase == 0 else ush_s.at[phase - 1]
                acc = acc + (src[base:base + rows_per_step, c0:c0 + LANES]
                             * dww_ref[tap:tap + 1, c0:c0 + LANES])
            c_s[r0:r0 + rows_per_step, c0:c0 + LANES] = acc

    y = _layer_norm(c_s[...], g_ref[...], b_ref[...])
    o_ref[...] = (y * jax.nn.sigmoid(y)).astype(BF16)


def _conv_branch(x2, w_ab, dw_w, dw_b, ln_g, ln_b, *, batch, seq, tm):
    n = batch * seq
    nt = seq // tm
    halo_per_tile = tm // CONV_HALO
    halo_per_seq = seq // CONV_HALO
    row = lambda b, j: (b * nt + j, 0)
    halo = lambda b, j: (jnp.maximum(b * halo_per_seq + j * halo_per_tile - 1, 0), 0)
    const = lambda b, j: (0, 0)
    return pl.pallas_call(
        functools.partial(_conv_kernel, tm=tm),
        grid=(batch, nt),
        in_specs=[
            pl.BlockSpec((tm, D_MODEL), row),
            pl.BlockSpec((CONV_HALO, D_MODEL), halo),
            pl.BlockSpec((D_MODEL, 2 * D_CONV), const),
            pl.BlockSpec((CONV_HALO, D_CONV), const),
            pl.BlockSpec((1, D_CONV), const),
            pl.BlockSpec((1, D_CONV), const),
            pl.BlockSpec((1, D_CONV), const),
        ],
        out_specs=pl.BlockSpec((tm, D_CONV), row),
        out_shape=jax.ShapeDtypeStruct((n, D_CONV), BF16),
        scratch_shapes=[
            pltpu.VMEM((CONV_HALO + tm, D_CONV), F32),
            pltpu.VMEM((SUBLANES - 1, CONV_HALO + tm, D_CONV), F32),
            pltpu.VMEM((tm, D_CONV), F32),
        ],
        compiler_params=pltpu.CompilerParams(
            dimension_semantics=("arbitrary", "arbitrary"), vmem_limit_bytes=VMEM_LIMIT),
        name="conv_branch",
    )(x2, x2, w_ab, dw_w, dw_b, ln_g, ln_b)


def _mixer_kernel(x_ref, cf_ref, at_ref, wg_ref, gb_ref, wco_ref, wao_ref, wout_ref, g_ref, b_ref, o_ref):
    x = x_ref[...]
    xb = x.astype(BF16)
    gate_c = jax.nn.sigmoid(_dot(xb, wg_ref[:, :D_MODEL]) + gb_ref[:, :D_MODEL])
    merged = gate_c * _dot(cf_ref[...], wco_ref[...])
    gate_a = jax.nn.sigmoid(_dot(xb, wg_ref[:, D_MODEL:]) + gb_ref[:, D_MODEL:])
    merged = merged + gate_a * _dot(at_ref[...], wao_ref[...])
    mixer = _dot(merged.astype(BF16), wout_ref[...])
    o_ref[...] = _layer_norm(DEEPNORM_ALPHA * x + mixer, g_ref[...], b_ref[...])


def _mixer(x2, cf, at, w_g, gate_b, w_co, w_ao, w_out, ln_g, ln_b, *, tm):
    n = x2.shape[0]
    row = lambda i: (i, 0)
    const = lambda i: (0, 0)
    return pl.pallas_call(
        _mixer_kernel,
        grid=(n // tm,),
        in_specs=[
            pl.BlockSpec((tm, D_MODEL), row),
            pl.BlockSpec((tm, D_CONV), row),
            pl.BlockSpec((tm, D_ATTN), row),
            pl.BlockSpec((D_MODEL, 2 * D_MODEL), const),
            pl.BlockSpec((1, 2 * D_MODEL), const),
            pl.BlockSpec((D_CONV, D_MODEL), const),
            pl.BlockSpec((D_ATTN, D_MODEL), const),
            pl.BlockSpec((D_MODEL, D_MODEL), const),
            pl.BlockSpec((1, D_MODEL), const),
            pl.BlockSpec((1, D_MODEL), const),
        ],
        out_specs=pl.BlockSpec((tm, D_MODEL), row),
        out_shape=jax.ShapeDtypeStruct((n, D_MODEL), F32),
        compiler_params=pltpu.CompilerParams(
            dimension_semantics=("arbitrary",), vmem_limit_bytes=VMEM_LIMIT),
        name="mixer",
    )(x2, cf, at, w_g, gate_b, w_co, w_ao, w_out, ln_g, ln_b)


def _ffn_kernel(x_ref, wi_ref, wo_ref, g_ref, b_ref, o_ref, *, ff_chunk):
    x = x_ref[...]
    xb = x.astype(BF16)
    acc = jnp.zeros(x.shape, F32)
    for c0 in range(0, D_FF, ff_chunk):
        h = jnp.maximum(_dot(xb, wi_ref[:, c0:c0 + ff_chunk]), 0.0)
        acc = acc + _dot((h * h).astype(BF16), wo_ref[c0:c0 + ff_chunk, :])
    o_ref[...] = _layer_norm(DEEPNORM_ALPHA * x + acc, g_ref[...], b_ref[...])


def _ffn(x2, w_i, w_o, ln_g, ln_b, *, tm, ff_chunk):
    n = x2.shape[0]
    row = lambda i: (i, 0)
    const = lambda i: (0, 0)
    return pl.pallas_call(
        functools.partial(_ffn_kernel, ff_chunk=ff_chunk),
        grid=(n // tm,),
        in_specs=[
            pl.BlockSpec((tm, D_MODEL), row),
            pl.BlockSpec((D_MODEL, D_FF), const, pipeline_mode=pl.Buffered(1)),
            pl.BlockSpec((D_FF, D_MODEL), const, pipeline_mode=pl.Buffered(1)),
            pl.BlockSpec((1, D_MODEL), const),
            pl.BlockSpec((1, D_MODEL), const),
        ],
        out_specs=pl.BlockSpec((tm, D_MODEL), row),
        out_shape=jax.ShapeDtypeStruct((n, D_MODEL), F32),
        compiler_params=pltpu.CompilerParams(
            dimension_semantics=("arbitrary",), vmem_limit_bytes=VMEM_LIMIT),
        name="ffn",
    )(x2, w_i, w_o, ln_g, ln_b)


def _rope_tables(seq):
    inv_freq = ROPE_THETA ** (-jnp.arange(0, HEAD_DIM, 2, dtype=F32) / HEAD_DIM)
    ang = jnp.arange(seq, dtype=jnp.int32).astype(F32)[:, None] * inv_freq[None, :]
    cos = jnp.cos(ang)
    sin = jnp.sin(ang)
    cos_blk = jnp.concatenate([cos, cos], axis=1)
    sin_blk = jnp.concatenate([-sin, sin], axis=1)
    return jnp.tile(cos_blk, (1, N_HEADS)), jnp.tile(sin_blk, (1, N_HEADS))


def _pad_lanes(v, width):
    return jnp.pad(v, (0, width - v.shape[0])).reshape(1, width)


def kernel(x, w_in, dw_w, dw_b, conv_ln_g, conv_ln_b, w_conv_out, idx_k_ln_g, idx_k_ln_b,
           w_attn_out, gate_b, w_out, ln1_g, ln1_b, w_ff_in, w_ff_out, ln2_g, ln2_b):
    batch, seq, _ = x.shape
    cos_t, sin_t = _rope_tables(seq)
    tm = min(512, seq)
    tq = min(256, seq)
    tk = min(512, seq)

    o_a, o_b, o_q, o_k, o_v = 0, D_CONV, 2 * D_CONV, 2 * D_CONV + D_ATTN, 2 * D_CONV + 2 * D_ATTN
    o_qi = o_v + D_ATTN
    o_ki = o_qi + N_IDX_HEADS * IDX_DIM
    o_wi = o_ki + IDX_DIM
    o_gc = o_wi + N_IDX_HEADS
    o_ga = o_gc + D_MODEL

    h = x.reshape(batch * seq, D_MODEL)
    for layer in range(w_in.shape[0]):
        w = w_in[layer]
        small = jnp.pad(w[:, o_ki:o_gc], ((0, 0), (0, LANES - (o_gc - o_ki))))
        w_attn = jnp.concatenate([w[:, o_q:o_ki], small], axis=1).astype(BF16)
        w_ab = w[:, o_a:o_q].astype(BF16)
        w_g = w[:, o_gc:o_ga + D_MODEL].astype(BF16)

        qt, k, vt, qit, ki, wit = _inproj(
            h, w_attn, cos_t, sin_t,
            _pad_lanes(idx_k_ln_g[layer], LANES), _pad_lanes(idx_k_ln_b[layer], LANES),
            batch=batch, seq=seq, tm=tk)
        attn = _dsa(qit, wit, ki, qt, k, vt, batch=batch, seq=seq, tq=tq, tk=tk)

        dww = jnp.pad(dw_w[layer], ((0, CONV_HALO - CONV_WIDTH), (0, 0)))
        conv_feat = _conv_branch(
            h, w_ab, dww, dw_b[layer].reshape(1, D_CONV),
            conv_ln_g[layer].reshape(1, D_CONV), conv_ln_b[layer].reshape(1, D_CONV),
            batch=batch, seq=seq, tm=tm)

        h = _mixer(h, conv_feat, attn, w_g, gate_b[layer].reshape(1, 2 * D_MODEL),
                   w_conv_out[layer].astype(BF16), w_attn_out[layer].astype(BF16),
                   w_out[layer].astype(BF16),
                   ln1_g[layer].reshape(1, D_MODEL), ln1_b[layer].reshape(1, D_MODEL), tm=tm)
        h = _ffn(h, w_ff_in[layer].astype(BF16), w_ff_out[layer].astype(BF16),
                 ln2_g[layer].reshape(1, D_MODEL), ln2_b[layer].reshape(1, D_MODEL),
                 tm=tm, ff_chunk=1024)
    return h.reshape(batch, seq, D_MODEL)
```

```python
import functools

import jax
import jax.numpy as jnp
from jax import lax
from jax.experimental import pallas as pl
from jax.experimental.pallas import tpu as pltpu

D_MODEL = 1024
CHUNK = 64
D_CONV = 512
CONV_WIDTH = 31
N_HEADS = 8
HEAD_DIM = 64
D_ATTN = N_HEADS * HEAD_DIM
N_IDX_HEADS = 8
IDX_DIM = 64
TOPK_MAX = 256
D_FF = 4 * D_MODEL
ROPE_THETA = 10000.0
LN_EPS = 1e-5
NEG_INF = -1e30
DEPTH = 1
DEEPNORM_ALPHA = (2.0 * DEPTH) ** 0.25

LANES = 128
SUBLANES = 8
CONV_HALO = 32
ONES_ROWS = 16
VT_ROWS = HEAD_DIM + ONES_ROWS
ACC_ROWS = 4 * SUBLANES
BISECT_PASSES = 24
SOFTMAX_ROWS = 128
LOG2_E = 1.4426950408889634
INT_MIN = -(2 ** 31)
KEY_NEG_FLT_MAX = -(2 ** 31) + 0x00800000
M_FLOOR = -1e20
VMEM_LIMIT = 56 * 1024 * 1024

F32 = jnp.float32
BF16 = jnp.bfloat16


def _dot(a, b):
    return jnp.dot(a, b, preferred_element_type=F32)


def _layer_norm(x, g, b):
    mu = jnp.mean(x, axis=-1, keepdims=True)
    d = x - mu
    var = jnp.mean(d * d, axis=-1, keepdims=True)
    return d * lax.rsqrt(var + LN_EPS) * g + b


def _rope(p, cos, sin_signed):
    width = p.shape[-1]
    lane = lax.broadcasted_iota(jnp.int32, p.shape, 1)
    first_half = (lane % HEAD_DIM) < (HEAD_DIM // 2)
    swapped = jnp.where(first_half,
                        pltpu.roll(p, width - HEAD_DIM // 2, 1),
                        pltpu.roll(p, HEAD_DIM // 2, 1))
    return p * cos + swapped * sin_signed


def _rope_transposed(pt, cos_t, sin_signed_t):
    half = HEAD_DIM // 2
    blocks = []
    for r0 in range(0, pt.shape[0], HEAD_DIM):
        blocks += [pt[r0 + half:r0 + HEAD_DIM], pt[r0:r0 + half]]
    return pt * cos_t + jnp.concatenate(blocks, axis=0) * sin_signed_t


def _inproj_kernel(x_ref, w_ref, cos_ref, sin_ref, cost_ref, sint_ref, kig_ref, kib_ref,
                   qt_ref, k_ref, vt_ref, qit_ref, ki_ref, wit_ref):
    xb = x_ref[...].astype(BF16)
    tm = xb.shape[0]
    cos = cos_ref[...]
    sin = sin_ref[...]
    cos_t = cost_ref[...]
    sin_t = sint_ref[...]
    qt = _rope_transposed(_dot(xb, w_ref[:, 0:D_ATTN]).T, cos_t, sin_t)
    qt_ref[...] = (qt * (HEAD_DIM ** -0.5 * LOG2_E)).astype(BF16)
    k_ref[...] = _rope(_dot(xb, w_ref[:, D_ATTN:2 * D_ATTN]), cos, sin).astype(BF16)

    vt = _dot(xb, w_ref[:, 2 * D_ATTN:3 * D_ATTN]).T
    ones = jnp.ones((ONES_ROWS, tm), F32)
    pieces = []
    for h in range(N_HEADS):
        pieces += [vt[h * HEAD_DIM:(h + 1) * HEAD_DIM], ones]
    vt_ref[...] = jnp.concatenate(pieces, axis=0).astype(BF16)

    qit = _rope_transposed(_dot(xb, w_ref[:, 3 * D_ATTN:4 * D_ATTN]).T, cos_t, sin_t)
    qit_ref[...] = (qit * (IDX_DIM ** -0.5)).astype(BF16)

    sm = _dot(xb, w_ref[:, 4 * D_ATTN:4 * D_ATTN + LANES])
    lane = lax.broadcasted_iota(jnp.int32, sm.shape, 1)
    is_ki = lane < IDX_DIM
    mu = jnp.sum(jnp.where(is_ki, sm, 0.0), axis=-1, keepdims=True) * (1.0 / IDX_DIM)
    d = jnp.where(is_ki, sm - mu, 0.0)
    var = jnp.sum(d * d, axis=-1, keepdims=True) * (1.0 / IDX_DIM)
    kn = d * lax.rsqrt(var + LN_EPS) * kig_ref[...] + kib_ref[...]
    kr = _rope(kn, cos[:, :LANES], sin[:, :LANES])
    ki_ref[...] = kr[:, :IDX_DIM].astype(BF16)
    wit_ref[...] = sm.T[IDX_DIM:IDX_DIM + N_IDX_HEADS] * (N_IDX_HEADS ** -0.5)


def _inproj(x2, w_attn, cos_t, sin_t, kig, kib, *, batch, seq, tm):
    n = batch * seq
    nt = seq // tm
    row = lambda j, b: (b * nt + j, 0)
    col = lambda j, b: (b, j)
    pos = lambda j, b: (j, 0)
    const = lambda j, b: (0, 0)
    wcols = w_attn.shape[1]
    return pl.pallas_call(
        _inproj_kernel,
        grid=(nt, batch),
        in_specs=[
            pl.BlockSpec((tm, D_MODEL), row),
            pl.BlockSpec((D_MODEL, wcols), const),
            pl.BlockSpec((tm, D_ATTN), pos),
            pl.BlockSpec((tm, D_ATTN), pos),
            pl.BlockSpec((D_ATTN, tm), lambda j, b: (0, j)),
            pl.BlockSpec((D_ATTN, tm), lambda j, b: (0, j)),
            pl.BlockSpec((1, LANES), const),
            pl.BlockSpec((1, LANES), const),
        ],
        out_specs=[
            pl.BlockSpec((D_ATTN, tm), col),
            pl.BlockSpec((tm, D_ATTN), row),
            pl.BlockSpec((None, None, N_HEADS * VT_ROWS, tm), lambda j, b: (b, j, 0, 0)),
            pl.BlockSpec((D_ATTN, tm), col),
            pl.BlockSpec((tm, IDX_DIM), row),
            pl.BlockSpec((N_IDX_HEADS, tm), col),
        ],
        out_shape=[
            jax.ShapeDtypeStruct((batch * D_ATTN, seq), BF16),
            jax.ShapeDtypeStruct((n, D_ATTN), BF16),
            jax.ShapeDtypeStruct((batch, nt, N_HEADS * VT_ROWS, tm), BF16),
            jax.ShapeDtypeStruct((batch * D_ATTN, seq), BF16),
            jax.ShapeDtypeStruct((n, IDX_DIM), BF16),
            jax.ShapeDtypeStruct((batch * N_IDX_HEADS, seq), F32),
        ],
        compiler_params=pltpu.CompilerParams(
            dimension_semantics=("arbitrary", "arbitrary"), vmem_limit_bytes=VMEM_LIMIT),
        name="inproj",
    )(x2, w_attn, cos_t, sin_t, cos_t.T, sin_t.T, kig, kib)


def _key_to_f32(key):
    bits = key ^ ((key >> 31) & jnp.int32(0x7FFFFFFF))
    return lax.bitcast_convert_type(bits, F32)


def _fold_rows(x, op):
    rows, cols = x.shape
    return op(x.reshape(rows // ACC_ROWS, ACC_ROWS, cols), axis=0)


def _dsa_kernel(qit_ref, wit_ref, ki_ref, qt_ref, k_ref, vt_ref, o_ref,
                isc_s, qz_s, thr_s, m_s, mblk_s, alpha_s, acc_s, s_st, p_st, *, tq, tk, topk, seq):
    j = pl.program_id(1)
    t0 = j * tq
    n_kb = (t0 + tq + tk - 1) // tk

    chunk_shift = CHUNK.bit_length() - 1
    q_chunk = (t0 + lax.broadcasted_iota(jnp.int32, (tk, tq), 1)) >> chunk_shift
    key_off = lax.broadcasted_iota(jnp.int32, (tk, tq), 0)

    def score_block(kb, carry):
        hi_part, lo_part = carry
        kib = ki_ref[pl.ds(pl.multiple_of(kb * tk, tk), tk), :]
        isc = jnp.zeros((tk, tq), F32)
        for h in range(N_IDX_HEADS):
            lg = _dot(kib, qit_ref[h * IDX_DIM:(h + 1) * IDX_DIM, :])
            isc = isc + jnp.maximum(lg, 0.0) * wit_ref[h:h + 1, :]
        admissible = ((kb * tk + key_off) >> chunk_shift) <= q_chunk
        isc_s[kb] = jnp.where(admissible, isc, -jnp.inf)
        return (jnp.maximum(hi_part, _fold_rows(isc, jnp.max)),
                jnp.minimum(lo_part, _fold_rows(isc, jnp.min)))

    hi_part, lo_part = lax.fori_loop(
        0, n_kb, score_block,
        (jnp.full((ACC_ROWS, tq), -jnp.inf, F32), jnp.full((ACC_ROWS, tq), jnp.inf, F32)))

    def count_keys(pred_fn):
        def body(kb, part):
            hit = jnp.where(pred_fn(kb, isc_s[kb]), 1, 0)
            return part + _fold_rows(hit, jnp.sum)
        part = lax.fori_loop(0, n_kb, body, jnp.zeros((ACC_ROWS, tq), jnp.int32))
        return jnp.sum(part, axis=0, keepdims=True)

    q_pos = t0 + lax.broadcasted_iota(jnp.int32, (1, tq), 1)
    n_admissible = ((q_pos >> chunk_shift) + 1) << chunk_shift

    def bisect_step(_, carry):
        lo, hi, n_lo = carry
        mid = lo + (hi - lo) * 0.5
        cnt = count_keys(lambda kb, blk: blk >= mid)
        keep = cnt >= topk
        return jnp.where(keep, mid, lo), jnp.where(keep, hi, mid), jnp.where(keep, cnt, n_lo)

    lo, _, n_lo = lax.fori_loop(
        0, BISECT_PASSES, bisect_step,
        (jnp.min(lo_part, axis=0, keepdims=True), jnp.max(hi_part, axis=0, keepdims=True), n_admissible))
    thr_s[...] = jnp.broadcast_to(lo, thr_s.shape)
    unresolved = jnp.max(n_lo) > topk

    def to_bias(select_fn):
        def body(kb, carry):
            isc_s[kb] = jnp.where(select_fn(kb, isc_s[kb]), 0.0, NEG_INF)
            return carry
        lax.fori_loop(0, n_kb, body, 0)

    @pl.when(jnp.logical_not(unresolved))
    def _():
        thr_v = thr_s[0:1, :]
        to_bias(lambda kb, blk: blk >= thr_v)

    @pl.when(unresolved)
    def _():
        def bit_step(i, carry):
            thr_key, n_ge = carry
            cand = jnp.where(i == 0, jnp.zeros_like(thr_key), thr_key | (jnp.int32(1) << (31 - i)))
            cand_f = _key_to_f32(jnp.maximum(cand, KEY_NEG_FLT_MAX))
            cnt = count_keys(lambda kb, blk: blk >= cand_f)
            accept = cnt >= topk
            return jnp.where(accept, cand, thr_key), jnp.where(accept, cnt, n_ge)

        thr_key, n_ge = lax.fori_loop(
            0, 32, bit_step, (jnp.full((1, tq), INT_MIN, jnp.int32), jnp.zeros((1, tq), jnp.int32)))
        thr_v = _key_to_f32(jnp.maximum(thr_key, KEY_NEG_FLT_MAX))
        has_ties = jnp.max(n_ge) > topk

        @pl.when(jnp.logical_not(has_ties))
        def _():
            to_bias(lambda kb, blk: blk >= thr_v)

        @pl.when(has_ties)
        def _():
            n_gt = count_keys(lambda kb, blk: blk > thr_v)
            need = topk - n_gt
            n_bits = seq.bit_length()

            def idx_step(i, lim):
                cand = lim | (jnp.int32(1) << (n_bits - 1 - i))
                cnt = count_keys(lambda kb, blk: (blk == thr_v) & (kb * tk + key_off < cand))
                return jnp.where(cnt < need, cand, lim)

            lim = lax.fori_loop(0, n_bits, idx_step, jnp.zeros((1, tq), jnp.int32))
            last_tie = jnp.where(n_ge > topk, lim, seq)
            to_bias(lambda kb, blk: (blk > thr_v) | ((blk == thr_v) & (kb * tk + key_off <= last_tie)))

    zeros_half = jnp.zeros((HEAD_DIM, tq), BF16)
    for h in range(N_HEADS):
        head_rows = qt_ref[h * HEAD_DIM:(h + 1) * HEAD_DIM, :]
        pair = [head_rows, zeros_half] if h % 2 == 0 else [zeros_half, head_rows]
        qz_s[h] = jnp.concatenate(pair, axis=0)
    m_s[...] = jnp.full(m_s.shape, M_FLOOR, F32)
    acc_s[...] = jnp.zeros(acc_s.shape, F32)

    chunks = [slice(c, c + SOFTMAX_ROWS) for c in range(0, tk, SOFTMAX_ROWS)]

    def scores(kb, h):
        rows = pl.ds(pl.multiple_of(kb * tk, tk), tk)
        slab = slice((h // 2) * LANES, (h // 2 + 1) * LANES)
        s = _dot(k_ref[rows, slab], qz_s[h]) + isc_s[kb]
        s_st[h] = s
        mblk_s[h] = jnp.broadcast_to(
            jnp.max(_fold_rows(s, jnp.max), axis=0, keepdims=True), (SUBLANES, tq))

    def probs(h):
        m_old = m_s[h]
        m_new = jnp.maximum(m_old, mblk_s[h])
        alpha_s[h] = jnp.exp2(m_old - m_new)
        for c in chunks:
            p_st[h, c, :] = jnp.exp2(s_st[h, c, :] - m_new[0:1, :]).astype(BF16)
        m_s[h] = m_new

    def accum(kb, h):
        pv = _dot(vt_ref[kb, h * VT_ROWS:(h + 1) * VT_ROWS, :], p_st[h])
        acc_s[h] = alpha_s[h][0:1, :] * acc_s[h] + pv

    for h in range(N_HEADS):
        scores(0, h)
    for h in range(N_HEADS):
        probs(h)

    @pl.when(n_kb >= 2)
    def _():
        for h in range(N_HEADS):
            scores(1, h)

    def steady(t, carry):
        for h in range(N_HEADS):
            accum(t - 2, h)
        for h in range(N_HEADS):
            probs(h)
        for h in range(N_HEADS):
            scores(t, h)
        return carry

    lax.fori_loop(2, n_kb, steady, 0)

    @pl.when(n_kb >= 2)
    def _():
        for h in range(N_HEADS):
            accum(n_kb - 2, h)
        for h in range(N_HEADS):
            probs(h)

    for h in range(N_HEADS):
        accum(n_kb - 1, h)

    outs = []
    for h in range(N_HEADS):
        a = acc_s[h]
        outs.append(a[:HEAD_DIM] / a[HEAD_DIM:HEAD_DIM + 1])
    o_ref[...] = jnp.concatenate(outs, axis=0).T.astype(BF16)


def _dsa(qit, wit, ki, qt, k, vt, *, batch, seq, tq, tk):
    n = batch * seq
    nt = seq // tq
    topk = min(TOPK_MAX, seq // 4)
    n_blk = seq // tk
    qcol = lambda b, j: (b, j)
    whole = lambda b, j: (b, 0)
    kern = functools.partial(_dsa_kernel, tq=tq, tk=tk, topk=topk, seq=seq)
    return pl.pallas_call(
        kern,
        grid=(batch, nt),
        in_specs=[
            pl.BlockSpec((D_ATTN, tq), qcol),
            pl.BlockSpec((N_IDX_HEADS, tq), qcol),
            pl.BlockSpec((seq, IDX_DIM), whole),
            pl.BlockSpec((D_ATTN, tq), qcol),
            pl.BlockSpec((seq, D_ATTN), whole),
            pl.BlockSpec((None, n_blk, N_HEADS * VT_ROWS, tk), lambda b, j: (b, 0, 0, 0)),
        ],
        out_specs=pl.BlockSpec((tq, D_ATTN), lambda b, j: (b * nt + j, 0)),
        out_shape=jax.ShapeDtypeStruct((n, D_ATTN), BF16),
        scratch_shapes=[
            pltpu.VMEM((n_blk, tk, tq), F32),
            pltpu.VMEM((N_HEADS, LANES, tq), BF16),
            pltpu.VMEM((SUBLANES, tq), F32),
            pltpu.VMEM((N_HEADS, SUBLANES, tq), F32),
            pltpu.VMEM((N_HEADS, SUBLANES, tq), F32),
            pltpu.VMEM((N_HEADS, SUBLANES, tq), F32),
            pltpu.VMEM((N_HEADS, VT_ROWS, tq), F32),
            pltpu.VMEM((N_HEADS, tk, tq), F32),
            pltpu.VMEM((N_HEADS, tk, tq), BF16),
        ],
        compiler_params=pltpu.CompilerParams(
            dimension_semantics=("arbitrary", "arbitrary"), vmem_limit_bytes=VMEM_LIMIT),
        name="dsa",
    )(qit, wit, ki, qt, k, vt)


def _conv_kernel(x_ref, xh_ref, wab_ref, dww_ref, dwb_ref, g_ref, b_ref, o_ref, u_s, ush_s, c_s, *, tm):
    j = pl.program_id(1)

    def glu(xb):
        ab = _dot(xb, wab_ref[...])
        return ab[:, :D_CONV] * jax.nn.sigmoid(ab[:, D_CONV:])

    u_halo = glu(xh_ref[...].astype(BF16))
    u_s[0:CONV_HALO, :] = jnp.where(j > 0, u_halo, 0.0)
    u_s[CONV_HALO:CONV_HALO + tm, :] = glu(x_ref[...].astype(BF16))

    shifted_rows = CONV_HALO + tm - SUBLANES
    for r in range(1, SUBLANES):
        ush_s[r - 1, 0:shifted_rows, :] = u_s[r:r + shifted_rows, :]

    rows_per_step = 64
    first = CONV_HALO - (CONV_WIDTH - 1)
    for r0 in range(0, tm, rows_per_step):
        for c0 in range(0, D_CONV, LANES):
            acc = jnp.broadcast_to(dwb_ref[:, c0:c0 + LANES], (rows_per_step, LANES))
            for tap in range(CONV_WIDTH):
                phase = (first + tap) % SUBLANES
                base = r0 + (first + tap) - phase
                src = u_s if phase == 0 else ush_s.at[phase - 1]
                acc = acc + (src[base:base + rows_per_step, c0:c0 + LANES]
                             * dww_ref[tap:tap + 1, c0:c0 + LANES])
            c_s[r0:r0 + rows_per_step, c0:c0 + LANES] = acc

    y = _layer_norm(c_s[...], g_ref[...], b_ref[...])
    o_ref[...] = (y * jax.nn.sigmoid(y)).astype(BF16)


def _conv_branch(x2, w_ab, dw_w, dw_b, ln_g, ln_b, *, batch, seq, tm):
    n = batch * seq
    nt = seq // tm
    halo_per_tile = tm // CONV_HALO
    halo_per_seq = seq // CONV_HALO
    row = lambda b, j: (b * nt + j, 0)
    halo = lambda b, j: (jnp.maximum(b * halo_per_seq + j * halo_per_tile - 1, 0), 0)
    const = lambda b, j: (0, 0)
    return pl.pallas_call(
        functools.partial(_conv_kernel, tm=tm),
        grid=(batch, nt),
        in_specs=[
            pl.BlockSpec((tm, D_MODEL), row),
            pl.BlockSpec((CONV_HALO, D_MODEL), halo),
            pl.BlockSpec((D_MODEL, 2 * D_CONV), const),
            pl.BlockSpec((CONV_HALO, D_CONV), const),
            pl.BlockSpec((1, D_CONV), const),
            pl.BlockSpec((1, D_CONV), const),
            pl.BlockSpec((1, D_CONV), const),
        ],
        out_specs=pl.BlockSpec((tm, D_CONV), row),
        out_shape=jax.ShapeDtypeStruct((n, D_CONV), BF16),
        scratch_shapes=[
            pltpu.VMEM((CONV_HALO + tm, D_CONV), F32),
            pltpu.VMEM((SUBLANES - 1, CONV_HALO + tm, D_CONV), F32),
            pltpu.VMEM((tm, D_CONV), F32),
        ],
        compiler_params=pltpu.CompilerParams(
            dimension_semantics=("arbitrary", "arbitrary"), vmem_limit_bytes=VMEM_LIMIT),
        name="conv_branch",
    )(x2, x2, w_ab, dw_w, dw_b, ln_g, ln_b)


def _mixer_kernel(x_ref, cf_ref, at_ref, wg_ref, gb_ref, wco_ref, wao_ref, wout_ref, g_ref, b_ref, o_ref):
    x = x_ref[...]
    xb = x.astype(BF16)
    gate_c = jax.nn.sigmoid(_dot(xb, wg_ref[:, :D_MODEL]) + gb_ref[:, :D_MODEL])
    merged = gate_c * _dot(cf_ref[...], wco_ref[...])
    gate_a = jax.nn.sigmoid(_dot(xb, wg_ref[:, D_MODEL:]) + gb_ref[:, D_MODEL:])
    merged = merged + gate_a * _dot(at_ref[...], wao_ref[...])
    mixer = _dot(merged.astype(BF16), wout_ref[...])
    o_ref[...] = _layer_norm(DEEPNORM_ALPHA * x + mixer, g_ref[...], b_ref[...])


def _mixer(x2, cf, at, w_g, gate_b, w_co, w_ao, w_out, ln_g, ln_b, *, tm):
    n = x2.shape[0]
    row = lambda i: (i, 0)
    const = lambda i: (0, 0)
    return pl.pallas_call(
        _mixer_kernel,
        grid=(n // tm,),
        in_specs=[
            pl.BlockSpec((tm, D_MODEL), row),
            pl.BlockSpec((tm, D_CONV), row),
            pl.BlockSpec((tm, D_ATTN), row),
            pl.BlockSpec((D_MODEL, 2 * D_MODEL), const),
            pl.BlockSpec((1, 2 * D_MODEL), const),
            pl.BlockSpec((D_CONV, D_MODEL), const),
            pl.BlockSpec((D_ATTN, D_MODEL), const),
            pl.BlockSpec((D_MODEL, D_MODEL), const),
            pl.BlockSpec((1, D_MODEL), const),
            pl.BlockSpec((1, D_MODEL), const),
        ],
        out_specs=pl.BlockSpec((tm, D_MODEL), row),
        out_shape=jax.ShapeDtypeStruct((n, D_MODEL), F32),
        compiler_params=pltpu.CompilerParams(
            dimension_semantics=("arbitrary",), vmem_limit_bytes=VMEM_LIMIT),
        name="mixer",
    )(x2, cf, at, w_g, gate_b, w_co, w_ao, w_out, ln_g, ln_b)


def _ffn_kernel(x_ref, wi_ref, wo_ref, g_ref, b_ref, o_ref, *, ff_chunk):
    x = x_ref[...]
    xb = x.astype(BF16)
    acc = jnp.zeros(x.shape, F32)
    for c0 in range(0, D_FF, ff_chunk):
        h = jnp.maximum(_dot(xb, wi_ref[:, c0:c0 + ff_chunk]), 0.0)
        acc = acc + _dot((h * h).astype(BF16), wo_ref[c0:c0 + ff_chunk, :])
    o_ref[...] = _layer_norm(DEEPNORM_ALPHA * x + acc, g_ref[...], b_ref[...])


def _ffn(x2, w_i, w_o, ln_g, ln_b, *, tm, ff_chunk):
    n = x2.shape[0]
    row = lambda i: (i, 0)
    const = lambda i: (0, 0)
    return pl.pallas_call(
        functools.partial(_ffn_kernel, ff_chunk=ff_chunk),
        grid=(n // tm,),
        in_specs=[
            pl.BlockSpec((tm, D_MODEL), row),
            pl.BlockSpec((D_MODEL, D_FF), const, pipeline_mode=pl.Buffered(1)),
            pl.BlockSpec((D_FF, D_MODEL), const, pipeline_mode=pl.Buffered(1)),
            pl.BlockSpec((1, D_MODEL), const),
            pl.BlockSpec((1, D_MODEL), const),
        ],
        out_specs=pl.BlockSpec((tm, D_MODEL), row),
        out_shape=jax.ShapeDtypeStruct((n, D_MODEL), F32),
        compiler_params=pltpu.CompilerParams(
            dimension_semantics=("arbitrary",), vmem_limit_bytes=VMEM_LIMIT),
        name="ffn",
    )(x2, w_i, w_o, ln_g, ln_b)


def _rope_tables(seq):
    inv_freq = ROPE_THETA ** (-jnp.arange(0, HEAD_DIM, 2, dtype=F32) / HEAD_DIM)
    ang = jnp.arange(seq, dtype=jnp.int32).astype(F32)[:, None] * inv_freq[None, :]
    cos = jnp.cos(ang)
    sin = jnp.sin(ang)
    cos_blk = jnp.concatenate([cos, cos], axis=1)
    sin_blk = jnp.concatenate([-sin, sin], axis=1)
    return jnp.tile(cos_blk, (1, N_HEADS)), jnp.tile(sin_blk, (1, N_HEADS))


def _pad_lanes(v, width):
    return jnp.pad(v, (0, width - v.shape[0])).reshape(1, width)


def kernel(x, w_in, dw_w, dw_b, conv_ln_g, conv_ln_b, w_conv_out, idx_k_ln_g, idx_k_ln_b,
           w_attn_out, gate_b, w_out, ln1_g, ln1_b, w_ff_in, w_ff_out, ln2_g, ln2_b):
    batch, seq, _ = x.shape
    cos_t, sin_t = _rope_tables(seq)
    tm = min(512, seq)
    tq = min(256, seq)
    tk = min(512, seq)

    o_a, o_b, o_q, o_k, o_v = 0, D_CONV, 2 * D_CONV, 2 * D_CONV + D_ATTN, 2 * D_CONV + 2 * D_ATTN
    o_qi = o_v + D_ATTN
    o_ki = o_qi + N_IDX_HEADS * IDX_DIM
    o_wi = o_ki + IDX_DIM
    o_gc = o_wi + N_IDX_HEADS
    o_ga = o_gc + D_MODEL

    h = x.reshape(batch * seq, D_MODEL)
    for layer in range(w_in.shape[0]):
        w = w_in[layer]
        small = jnp.pad(w[:, o_ki:o_gc], ((0, 0), (0, LANES - (o_gc - o_ki))))
        w_attn = jnp.concatenate([w[:, o_q:o_ki], small], axis=1).astype(BF16)
        w_ab = w[:, o_a:o_q].astype(BF16)
        w_g = w[:, o_gc:o_ga + D_MODEL].astype(BF16)

        qt, k, vt, qit, ki, wit = _inproj(
            h, w_attn, cos_t, sin_t,
            _pad_lanes(idx_k_ln_g[layer], LANES), _pad_lanes(idx_k_ln_b[layer], LANES),
            batch=batch, seq=seq, tm=tk)
        attn = _dsa(qit, wit, ki, qt, k, vt, batch=batch, seq=seq, tq=tq, tk=tk)

        dww = jnp.pad(dw_w[layer], ((0, CONV_HALO - CONV_WIDTH), (0, 0)))
        conv_feat = _conv_branch(
            h, w_ab, dww, dw_b[layer].reshape(1, D_CONV),
            conv_ln_g[layer].reshape(1, D_CONV), conv_ln_b[layer].reshape(1, D_CONV),
            batch=batch, seq=seq, tm=tm)

        h = _mixer(h, conv_feat, attn, w_g, gate_b[layer].reshape(1, 2 * D_MODEL),
                   w_conv_out[layer].astype(BF16), w_attn_out[layer].astype(BF16),
                   w_out[layer].astype(BF16),
                   ln1_g[layer].reshape(1, D_MODEL), ln1_b[layer].reshape(1, D_MODEL), tm=tm)
        h = _ffn(h, w_ff_in[layer].astype(BF16), w_ff_out[layer].astype(BF16),
                 ln2_g[layer].reshape(1, D_MODEL), ln2_b[layer].reshape(1, D_MODEL),
                 tm=tm, ff_chunk=1024)
    return h.reshape(batch, seq, D_MODEL)
```

```python
import functools

import jax
import jax.numpy as jnp
from jax import lax
from jax.experimental import pallas as pl
from jax.experimental.pallas import tpu as pltpu

D_MODEL = 1024
CHUNK = 64
D_CONV = 512
CONV_WIDTH = 31
N_HEADS = 8
HEAD_DIM = 64
D_ATTN = N_HEADS * HEAD_DIM
N_IDX_HEADS = 8
IDX_DIM = 64
TOPK_MAX = 256
D_FF = 4 * D_MODEL
ROPE_THETA = 10000.0
LN_EPS = 1e-5
NEG_INF = -1e30
DEPTH = 1
DEEPNORM_ALPHA = (2.0 * DEPTH) ** 0.25

LANES = 128
SUBLANES = 8
CONV_HALO = 32
ONES_ROWS = 16
VT_ROWS = HEAD_DIM + ONES_ROWS
ACC_ROWS = 4 * SUBLANES
BISECT_PASSES = 24
SOFTMAX_ROWS = 128
LOG2_E = 1.4426950408889634
INT_MIN = -(2 ** 31)
KEY_NEG_FLT_MAX = -(2 ** 31) + 0x00800000
M_FLOOR = -1e20
VMEM_LIMIT = 56 * 1024 * 1024

F32 = jnp.float32
BF16 = jnp.bfloat16


def _dot(a, b):
    return jnp.dot(a, b, preferred_element_type=F32)


def _layer_norm(x, g, b):
    mu = jnp.mean(x, axis=-1, keepdims=True)
    d = x - mu
    var = jnp.mean(d * d, axis=-1, keepdims=True)
    return d * lax.rsqrt(var + LN_EPS) * g + b


def _rope(p, cos, sin_signed):
    width = p.shape[-1]
    lane = lax.broadcasted_iota(jnp.int32, p.shape, 1)
    first_half = (lane % HEAD_DIM) < (HEAD_DIM // 2)
    swapped = jnp.where(first_half,
                        pltpu.roll(p, width - HEAD_DIM // 2, 1),
                        pltpu.roll(p, HEAD_DIM // 2, 1))
    return p * cos + swapped * sin_signed


def _rope_transposed(pt, cos_t, sin_signed_t):
    half = HEAD_DIM // 2
    blocks = []
    for r0 in range(0, pt.shape[0], HEAD_DIM):
        blocks += [pt[r0 + half:r0 + HEAD_DIM], pt[r0:r0 + half]]
    return pt * cos_t + jnp.concatenate(blocks, axis=0) * sin_signed_t


def _inproj_kernel(x_ref, w_ref, cos_ref, sin_ref, cost_ref, sint_ref, kig_ref, kib_ref,
                   qt_ref, k_ref, vt_ref, qit_ref, ki_ref, wit_ref):
    xb = x_ref[...].astype(BF16)
    tm = xb.shape[0]
    cos = cos_ref[...]
    sin = sin_ref[...]
    cos_t = cost_ref[...]
    sin_t = sint_ref[...]
    qt = _rope_transposed(_dot(xb, w_ref[:, 0:D_ATTN]).T, cos_t, sin_t)
    qt_ref[...] = (qt * (HEAD_DIM ** -0.5 * LOG2_E)).astype(BF16)
    k_ref[...] = _rope(_dot(xb, w_ref[:, D_ATTN:2 * D_ATTN]), cos, sin).astype(BF16)

    vt = _dot(xb, w_ref[:, 2 * D_ATTN:3 * D_ATTN]).T
    ones = jnp.ones((ONES_ROWS, tm), F32)
    pieces = []
    for h in range(N_HEADS):
        pieces += [vt[h * HEAD_DIM:(h + 1) * HEAD_DIM], ones]
    vt_ref[...] = jnp.concatenate(pieces, axis=0).astype(BF16)

    qit = _rope_transposed(_dot(xb, w_ref[:, 3 * D_ATTN:4 * D_ATTN]).T, cos_t, sin_t)
    qit_ref[...] = (qit * (IDX_DIM ** -0.5)).astype(BF16)

    sm = _dot(xb, w_ref[:, 4 * D_ATTN:4 * D_ATTN + LANES])
    lane = lax.broadcasted_iota(jnp.int32, sm.shape, 1)
    is_ki = lane < IDX_DIM
    mu = jnp.sum(jnp.where(is_ki, sm, 0.0), axis=-1, keepdims=True) * (1.0 / IDX_DIM)
    d = jnp.where(is_ki, sm - mu, 0.0)
    var = jnp.sum(d * d, axis=-1, keepdims=True) * (1.0 / IDX_DIM)
    kn = d * lax.rsqrt(var + LN_EPS) * kig_ref[...] + kib_ref[...]
    kr = _rope(kn, cos[:, :LANES], sin[:, :LANES])
    ki_ref[...] = kr[:, :IDX_DIM].astype(BF16)
    wit_ref[...] = sm.T[IDX_DIM:IDX_DIM + N_IDX_HEADS] * (N_IDX_HEADS ** -0.5)


def _inproj(x2, w_attn, cos_t, sin_t, kig, kib, *, batch, seq, tm):
    n = batch * seq
    nt = seq // tm
    row = lambda j, b: (b * nt + j, 0)
    col = lambda j, b: (b, j)
    pos = lambda j, b: (j, 0)
    const = lambda j, b: (0, 0)
    wcols = w_attn.shape[1]
    return pl.pallas_call(
        _inproj_kernel,
        grid=(nt, batch),
        in_specs=[
            pl.BlockSpec((tm, D_MODEL), row),
            pl.BlockSpec((D_MODEL, wcols), const),
            pl.BlockSpec((tm, D_ATTN), pos),
            pl.BlockSpec((tm, D_ATTN), pos),
            pl.BlockSpec((D_ATTN, tm), lambda j, b: (0, j)),
            pl.BlockSpec((D_ATTN, tm), lambda j, b: (0, j)),
            pl.BlockSpec((1, LANES), const),
            pl.BlockSpec((1, LANES), const),
        ],
        out_specs=[
            pl.BlockSpec((D_ATTN, tm), col),
            pl.BlockSpec((tm, D_ATTN), row),
            pl.BlockSpec((None, None, N_HEADS * VT_ROWS, tm), lambda j, b: (b, j, 0, 0)),
            pl.BlockSpec((D_ATTN, tm), col),
            pl.BlockSpec((tm, IDX_DIM), row),
            pl.BlockSpec((N_IDX_HEADS, tm), col),
        ],
        out_shape=[
            jax.ShapeDtypeStruct((batch * D_ATTN, seq), BF16),
            jax.ShapeDtypeStruct((n, D_ATTN), BF16),
            jax.ShapeDtypeStruct((batch, nt, N_HEADS * VT_ROWS, tm), BF16),
            jax.ShapeDtypeStruct((batch * D_ATTN, seq), BF16),
            jax.ShapeDtypeStruct((n, IDX_DIM), BF16),
            jax.ShapeDtypeStruct((batch * N_IDX_HEADS, seq), F32),
        ],
        compiler_params=pltpu.CompilerParams(
            dimension_semantics=("arbitrary", "arbitrary"), vmem_limit_bytes=VMEM_LIMIT),
        name="inproj",
    )(x2, w_attn, cos_t, sin_t, cos_t.T, sin_t.T, kig, kib)


def _key_to_f32(key):
    bits = key ^ ((key >> 31) & jnp.int32(0x7FFFFFFF))
    return lax.bitcast_convert_type(bits, F32)


def _fold_rows(x, op):
    rows, cols = x.shape
    return op(x.reshape(rows // ACC_ROWS, ACC_ROWS, cols), axis=0)


def _dsa_kernel(qit_ref, wit_ref, ki_ref, qt_ref, k_ref, vt_ref, o_ref,
                isc_s, qz_s, thr_s, need_s, m_s, mblk_s, alpha_s, acc_s, s_st, p_st, *, tq, tk, topk, seq):
    j = pl.program_id(1)
    t0 = j * tq
    n_kb = (t0 + tq + tk - 1) // tk

    chunk_shift = CHUNK.bit_length() - 1
    q_chunk = (t0 + lax.broadcasted_iota(jnp.int32, (tk, tq), 1)) >> chunk_shift
    key_off = lax.broadcasted_iota(jnp.int32, (tk, tq), 0)

    def score_block(kb, carry):
        hi_part, lo_part, pos_part, nonneg_part = carry
        kib = ki_ref[pl.ds(pl.multiple_of(kb * tk, tk), tk), :]
        isc = jnp.zeros((tk, tq), F32)
        for h in range(N_IDX_HEADS):
            lg = _dot(kib, qit_ref[h * IDX_DIM:(h + 1) * IDX_DIM, :])
            isc = isc + jnp.maximum(lg, 0.0) * wit_ref[h:h + 1, :]
        admissible = ((kb * tk + key_off) >> chunk_shift) <= q_chunk
        masked = jnp.where(admissible, isc, -jnp.inf)
        isc_s[kb] = masked
        return (jnp.maximum(hi_part, _fold_rows(isc, jnp.max)),
                jnp.minimum(lo_part, _fold_rows(isc, jnp.min)),
                pos_part + _fold_rows(jnp.where(masked > 0.0, 1, 0), jnp.sum),
                nonneg_part + _fold_rows(jnp.where(masked >= 0.0, 1, 0), jnp.sum))

    hi_part, lo_part, pos_part, nonneg_part = lax.fori_loop(
        0, n_kb, score_block,
        (jnp.full((ACC_ROWS, tq), -jnp.inf, F32), jnp.full((ACC_ROWS, tq), jnp.inf, F32),
         jnp.zeros((ACC_ROWS, tq), jnp.int32), jnp.zeros((ACC_ROWS, tq), jnp.int32)))
    s_max = jnp.max(hi_part, axis=0, keepdims=True)
    s_min = jnp.min(lo_part, axis=0, keepdims=True)
    n_pos = jnp.sum(pos_part, axis=0, keepdims=True)
    n_nonneg = jnp.sum(nonneg_part, axis=0, keepdims=True)

    def count_keys(pred_fn):
        def body(kb, part):
            hit = jnp.where(pred_fn(kb, isc_s[kb]), 1, 0)
            return part + _fold_rows(hit, jnp.sum)
        part = lax.fori_loop(0, n_kb, body, jnp.zeros((ACC_ROWS, tq), jnp.int32))
        return jnp.sum(part, axis=0, keepdims=True)

    q_pos = t0 + lax.broadcasted_iota(jnp.int32, (1, tq), 1)
    n_admissible = ((q_pos >> chunk_shift) + 1) << chunk_shift
    keep_all = n_admissible <= topk
    zero_thr = jnp.logical_and(jnp.logical_not(keep_all),
                               jnp.logical_and(n_pos < topk, n_nonneg >= topk))
    pinned = jnp.logical_or(keep_all, zero_thr)
    positive = n_pos >= topk
    pin = jnp.where(keep_all, s_min, 0.0)
    lo = jnp.where(pinned, pin, jnp.where(positive, 0.0, s_min))
    hi = jnp.where(pinned, pin, jnp.where(positive, s_max, 0.0))
    n_lo = jnp.where(jnp.logical_or(keep_all, jnp.logical_not(jnp.logical_or(zero_thr, positive))),
                     n_admissible, n_nonneg)

    def bisect_step(_, carry):
        lo, hi, n_lo = carry
        mid = lo + (hi - lo) * 0.5
        cnt = count_keys(lambda kb, blk: blk >= mid)
        keep = cnt >= topk
        return jnp.where(keep, mid, lo), jnp.where(keep, hi, mid), jnp.where(keep, cnt, n_lo)

    lo, _, n_lo = lax.fori_loop(0, BISECT_PASSES, bisect_step, (lo, hi, n_lo))
    thr_s[...] = jnp.broadcast_to(lo, thr_s.shape)
    unresolved = jnp.max(jnp.where(jnp.logical_or(zero_thr, n_lo <= topk), 0, 1)) > 0
    zero_ties = jnp.logical_and(zero_thr, n_nonneg > topk)
    need_s[...] = jnp.broadcast_to(jnp.where(zero_ties, topk - n_pos, seq), need_s.shape)
    tied = jnp.max(jnp.where(zero_ties, 1, 0)) > 0

    def to_bias(select_fn):
        def body(kb, carry):
            isc_s[kb] = jnp.where(select_fn(kb, isc_s[kb]), 0.0, NEG_INF)
            return carry
        lax.fori_loop(0, n_kb, body, 0)

    def to_bias_ranked(thr_v, need):
        lower_tri = jnp.where(lax.broadcasted_iota(jnp.int32, (tk, tk), 0)
                              >= lax.broadcasted_iota(jnp.int32, (tk, tk), 1), 1.0, 0.0).astype(BF16)
        need_f = need.astype(F32)

        def body(kb, before):
            blk = isc_s[kb]
            tie = blk == thr_v
            rank = before + _dot(lower_tri, jnp.where(tie, 1.0, 0.0).astype(BF16))
            keep = jnp.logical_or(blk > thr_v, jnp.logical_and(tie, rank <= need_f))
            isc_s[kb] = jnp.where(keep, 0.0, NEG_INF)
            return rank[tk - 1:tk, :]
        lax.fori_loop(0, n_kb, body, jnp.zeros((1, tq), F32))

    @pl.when(jnp.logical_not(jnp.logical_or(unresolved, tied)))
    def _():
        thr_v = thr_s[0:1, :]
        to_bias(lambda kb, blk: blk >= thr_v)

    @pl.when(jnp.logical_and(jnp.logical_not(unresolved), tied))
    def _():
        to_bias_ranked(thr_s[0:1, :], need_s[0:1, :])

    @pl.when(unresolved)
    def _():
        def bit_step(i, carry):
            thr_key, n_ge = carry
            cand = jnp.where(i == 0, jnp.zeros_like(thr_key), thr_key | (jnp.int32(1) << (31 - i)))
            cand_f = _key_to_f32(jnp.maximum(cand, KEY_NEG_FLT_MAX))
            cnt = count_keys(lambda kb, blk: blk >= cand_f)
            accept = cnt >= topk
            return jnp.where(accept, cand, thr_key), jnp.where(accept, cnt, n_ge)

        thr_key, n_ge = lax.fori_loop(
            0, 32, bit_step, (jnp.full((1, tq), INT_MIN, jnp.int32), jnp.zeros((1, tq), jnp.int32)))
        thr_v = _key_to_f32(jnp.maximum(thr_key, KEY_NEG_FLT_MAX))
        has_ties = jnp.max(n_ge) > topk

        @pl.when(jnp.logical_not(has_ties))
        def _():
            to_bias(lambda kb, blk: blk >= thr_v)

        @pl.when(has_ties)
        def _():
            n_gt = count_keys(lambda kb, blk: blk > thr_v)
            to_bias_ranked(thr_v, jnp.where(n_ge > topk, topk - n_gt, seq))

    zeros_half = jnp.zeros((HEAD_DIM, tq), BF16)
    for h in range(N_HEADS):
        head_rows = qt_ref[h * HEAD_DIM:(h + 1) * HEAD_DIM, :]
        pair = [head_rows, zeros_half] if h % 2 == 0 else [zeros_half, head_rows]
        qz_s[h] = jnp.concatenate(pair, axis=0)
    m_s[...] = jnp.full(m_s.shape, M_FLOOR, F32)
    acc_s[...] = jnp.zeros(acc_s.shape, F32)

    chunks = [slice(c, c + SOFTMAX_ROWS) for c in range(0, tk, SOFTMAX_ROWS)]

    def scores(kb, h):
        rows = pl.ds(pl.multiple_of(kb * tk, tk), tk)
        slab = slice((h // 2) * LANES, (h // 2 + 1) * LANES)
        s = _dot(k_ref[rows, slab], qz_s[h]) + isc_s[kb]
        s_st[h] = s
        mblk_s[h] = jnp.broadcast_to(
            jnp.max(_fold_rows(s, jnp.max), axis=0, keepdims=True), (SUBLANES, tq))

    def probs(h):
        m_old = m_s[h]
        m_new = jnp.maximum(m_old, mblk_s[h])
        alpha_s[h] = jnp.exp2(m_old - m_new)
        for c in chunks:
            p_st[h, c, :] = jnp.exp2(s_st[h, c, :] - m_new[0:1, :]).astype(BF16)
        m_s[h] = m_new

    def accum(kb, h):
        pv = _dot(vt_ref[kb, h * VT_ROWS:(h + 1) * VT_ROWS, :], p_st[h])
        acc_s[h] = alpha_s[h][0:1, :] * acc_s[h] + pv

    for h in range(N_HEADS):
        scores(0, h)
    for h in range(N_HEADS):
        probs(h)

    @pl.when(n_kb >= 2)
    def _():
        for h in range(N_HEADS):
            scores(1, h)

    def steady(t, carry):
        for h in range(N_HEADS):
            accum(t - 2, h)
        for h in range(N_HEADS):
            probs(h)
        for h in range(N_HEADS):
            scores(t, h)
        return carry

    lax.fori_loop(2, n_kb, steady, 0)

    @pl.when(n_kb >= 2)
    def _():
        for h in range(N_HEADS):
            accum(n_kb - 2, h)
        for h in range(N_HEADS):
            probs(h)

    for h in range(N_HEADS):
        accum(n_kb - 1, h)

    outs = []
    for h in range(N_HEADS):
        a = acc_s[h]
        outs.append(a[:HEAD_DIM] / a[HEAD_DIM:HEAD_DIM + 1])
    o_ref[...] = jnp.concatenate(outs, axis=0).T.astype(BF16)


def _dsa(qit, wit, ki, qt, k, vt, *, batch, seq, tq, tk):
    n = batch * seq
    nt = seq // tq
    topk = min(TOPK_MAX, seq // 4)
    n_blk = seq // tk
    qcol = lambda b, j: (b, j)
    whole = lambda b, j: (b, 0)
    kern = functools.partial(_dsa_kernel, tq=tq, tk=tk, topk=topk, seq=seq)
    return pl.pallas_call(
        kern,
        grid=(batch, nt),
        in_specs=[
            pl.BlockSpec((D_ATTN, tq), qcol),
            pl.BlockSpec((N_IDX_HEADS, tq), qcol),
            pl.BlockSpec((seq, IDX_DIM), whole),
            pl.BlockSpec((D_ATTN, tq), qcol),
            pl.BlockSpec((seq, D_ATTN), whole),
            pl.BlockSpec((None, n_blk, N_HEADS * VT_ROWS, tk), lambda b, j: (b, 0, 0, 0)),
        ],
        out_specs=pl.BlockSpec((tq, D_ATTN), lambda b, j: (b * nt + j, 0)),
        out_shape=jax.ShapeDtypeStruct((n, D_ATTN), BF16),
        scratch_shapes=[
            pltpu.VMEM((n_blk, tk, tq), F32),
            pltpu.VMEM((N_HEADS, LANES, tq), BF16),
            pltpu.VMEM((SUBLANES, tq), F32),
            pltpu.VMEM((SUBLANES, tq), jnp.int32),
            pltpu.VMEM((N_HEADS, SUBLANES, tq), F32),
            pltpu.VMEM((N_HEADS, SUBLANES, tq), F32),
            pltpu.VMEM((N_HEADS, SUBLANES, tq), F32),
            pltpu.VMEM((N_HEADS, VT_ROWS, tq), F32),
            pltpu.VMEM((N_HEADS, tk, tq), F32),
            pltpu.VMEM((N_HEADS, tk, tq), BF16),
        ],
        compiler_params=pltpu.CompilerParams(
            dimension_semantics=("arbitrary", "arbitrary"), vmem_limit_bytes=VMEM_LIMIT),
        name="dsa",
    )(qit, wit, ki, qt, k, vt)


def _conv_kernel(x_ref, xh_ref, wab_ref, dww_ref, dwb_ref, g_ref, b_ref, o_ref, u_s, ush_s, c_s, *, tm):
    j = pl.program_id(1)

    def glu(xb):
        ab = _dot(xb, wab_ref[...])
        return ab[:, :D_CONV] * jax.nn.sigmoid(ab[:, D_CONV:])

    u_halo = glu(xh_ref[...].astype(BF16))
    u_s[0:CONV_HALO, :] = jnp.where(j > 0, u_halo, 0.0)
    u_s[CONV_HALO:CONV_HALO + tm, :] = glu(x_ref[...].astype(BF16))

    shifted_rows = CONV_HALO + tm - SUBLANES
    for r in range(1, SUBLANES):
        ush_s[r - 1, 0:shifted_rows, :] = u_s[r:r + shifted_rows, :]

    rows_per_step = 64
    first = CONV_HALO - (CONV_WIDTH - 1)
    for r0 in range(0, tm, rows_per_step):
        for c0 in range(0, D_CONV, LANES):
            acc = jnp.broadcast_to(dwb_ref[:, c0:c0 + LANES], (rows_per_step, LANES))
            for tap in range(CONV_WIDTH):
                phase = (first + tap) % SUBLANES
                base = r0 + (first + tap) - phase
                src = u_s if phase == 0 else ush_s.at[phase - 1]
                acc = acc + (src[base:base + rows_per_step, c0:c0 + LANES]
                             * dww_ref[tap:tap + 1, c0:c0 + LANES])
            c_s[r0:r0 + rows_per_step, c0:c0 + LANES] = acc

    y = _layer_norm(c_s[...], g_ref[...], b_ref[...])
    o_ref[...] = (y * jax.nn.sigmoid(y)).astype(BF16)


def _conv_branch(x2, w_ab, dw_w, dw_b, ln_g, ln_b, *, batch, seq, tm):
    n = batch * seq
    nt = seq // tm
    halo_per_tile = tm // CONV_HALO
    halo_per_seq = seq // CONV_HALO
    row = lambda b, j: (b * nt + j, 0)
    halo = lambda b, j: (jnp.maximum(b * halo_per_seq + j * halo_per_tile - 1, 0), 0)
    const = lambda b, j: (0, 0)
    return pl.pallas_call(
        functools.partial(_conv_kernel, tm=tm),
        grid=(batch, nt),
        in_specs=[
            pl.BlockSpec((tm, D_MODEL), row),
            pl.BlockSpec((CONV_HALO, D_MODEL), halo),
            pl.BlockSpec((D_MODEL, 2 * D_CONV), const),
            pl.BlockSpec((CONV_HALO, D_CONV), const),
            pl.BlockSpec((1, D_CONV), const),
            pl.BlockSpec((1, D_CONV), const),
            pl.BlockSpec((1, D_CONV), const),
        ],
        out_specs=pl.BlockSpec((tm, D_CONV), row),
        out_shape=jax.ShapeDtypeStruct((n, D_CONV), BF16),
        scratch_shapes=[
            pltpu.VMEM((CONV_HALO + tm, D_CONV), F32),
            pltpu.VMEM((SUBLANES - 1, CONV_HALO + tm, D_CONV), F32),
            pltpu.VMEM((tm, D_CONV), F32),
        ],
        compiler_params=pltpu.CompilerParams(
            dimension_semantics=("arbitrary", "arbitrary"), vmem_limit_bytes=VMEM_LIMIT),
        name="conv_branch",
    )(x2, x2, w_ab, dw_w, dw_b, ln_g, ln_b)


def _mixer_kernel(x_ref, cf_ref, at_ref, wg_ref, gb_ref, wco_ref, wao_ref, wout_ref, g_ref, b_ref, o_ref):
    x = x_ref[...]
    xb = x.astype(BF16)
    gate_c = jax.nn.sigmoid(_dot(xb, wg_ref[:, :D_MODEL]) + gb_ref[:, :D_MODEL])
    merged = gate_c * _dot(cf_ref[...], wco_ref[...])
    gate_a = jax.nn.sigmoid(_dot(xb, wg_ref[:, D_MODEL:]) + gb_ref[:, D_MODEL:])
    merged = merged + gate_a * _dot(at_ref[...], wao_ref[...])
    mixer = _dot(merged.astype(BF16), wout_ref[...])
    o_ref[...] = _layer_norm(DEEPNORM_ALPHA * x + mixer, g_ref[...], b_ref[...])


def _mixer(x2, cf, at, w_g, gate_b, w_co, w_ao, w_out, ln_g, ln_b, *, tm):
    n = x2.shape[0]
    row = lambda i: (i, 0)
    const = lambda i: (0, 0)
    return pl.pallas_call(
        _mixer_kernel,
        grid=(n // tm,),
        in_specs=[
            pl.BlockSpec((tm, D_MODEL), row),
            pl.BlockSpec((tm, D_CONV), row),
            pl.BlockSpec((tm, D_ATTN), row),
            pl.BlockSpec((D_MODEL, 2 * D_MODEL), const),
            pl.BlockSpec((1, 2 * D_MODEL), const),
            pl.BlockSpec((D_CONV, D_MODEL), const),
            pl.BlockSpec((D_ATTN, D_MODEL), const),
            pl.BlockSpec((D_MODEL, D_MODEL), const),
            pl.BlockSpec((1, D_MODEL), const),
            pl.BlockSpec((1, D_MODEL), const),
        ],
        out_specs=pl.BlockSpec((tm, D_MODEL), row),
        out_shape=jax.ShapeDtypeStruct((n, D_MODEL), F32),
        compiler_params=pltpu.CompilerParams(
            dimension_semantics=("arbitrary",), vmem_limit_bytes=VMEM_LIMIT),
        name="mixer",
    )(x2, cf, at, w_g, gate_b, w_co, w_ao, w_out, ln_g, ln_b)


def _ffn_kernel(x_ref, wi_ref, wo_ref, g_ref, b_ref, o_ref, *, ff_chunk):
    x = x_ref[...]
    xb = x.astype(BF16)
    acc = jnp.zeros(x.shape, F32)
    for c0 in range(0, D_FF, ff_chunk):
        h = jnp.maximum(_dot(xb, wi_ref[:, c0:c0 + ff_chunk]), 0.0)
        acc = acc + _dot((h * h).astype(BF16), wo_ref[c0:c0 + ff_chunk, :])
    o_ref[...] = _layer_norm(DEEPNORM_ALPHA * x + acc, g_ref[...], b_ref[...])


def _ffn(x2, w_i, w_o, ln_g, ln_b, *, tm, ff_chunk):
    n = x2.shape[0]
    row = lambda i: (i, 0)
    const = lambda i: (0, 0)
    return pl.pallas_call(
        functools.partial(_ffn_kernel, ff_chunk=ff_chunk),
        grid=(n // tm,),
        in_specs=[
            pl.BlockSpec((tm, D_MODEL), row),
            pl.BlockSpec((D_MODEL, D_FF), const, pipeline_mode=pl.Buffered(1)),
            pl.BlockSpec((D_FF, D_MODEL), const, pipeline_mode=pl.Buffered(1)),
            pl.BlockSpec((1, D_MODEL), const),
            pl.BlockSpec((1, D_MODEL), const),
        ],
        out_specs=pl.BlockSpec((tm, D_MODEL), row),
        out_shape=jax.ShapeDtypeStruct((n, D_MODEL), F32),
        compiler_params=pltpu.CompilerParams(
            dimension_semantics=("arbitrary",), vmem_limit_bytes=VMEM_LIMIT),
        name="ffn",
    )(x2, w_i, w_o, ln_g, ln_b)


def _rope_tables(seq):
    inv_freq = ROPE_THETA ** (-jnp.arange(0, HEAD_DIM, 2, dtype=F32) / HEAD_DIM)
    ang = jnp.arange(seq, dtype=jnp.int32).astype(F32)[:, None] * inv_freq[None, :]
    cos = jnp.cos(ang)
    sin = jnp.sin(ang)
    cos_blk = jnp.concatenate([cos, cos], axis=1)
    sin_blk = jnp.concatenate([-sin, sin], axis=1)
    return jnp.tile(cos_blk, (1, N_HEADS)), jnp.tile(sin_blk, (1, N_HEADS))


def _pad_lanes(v, width):
    return jnp.pad(v, (0, width - v.shape[0])).reshape(1, width)


def kernel(x, w_in, dw_w, dw_b, conv_ln_g, conv_ln_b, w_conv_out, idx_k_ln_g, idx_k_ln_b,
           w_attn_out, gate_b, w_out, ln1_g, ln1_b, w_ff_in, w_ff_out, ln2_g, ln2_b):
    batch, seq, _ = x.shape
    cos_t, sin_t = _rope_tables(seq)
    tm = min(512, seq)
    tq = min(256, seq)
    tk = min(512, seq)

    o_a, o_b, o_q, o_k, o_v = 0, D_CONV, 2 * D_CONV, 2 * D_CONV + D_ATTN, 2 * D_CONV + 2 * D_ATTN
    o_qi = o_v + D_ATTN
    o_ki = o_qi + N_IDX_HEADS * IDX_DIM
    o_wi = o_ki + IDX_DIM
    o_gc = o_wi + N_IDX_HEADS
    o_ga = o_gc + D_MODEL

    h = x.reshape(batch * seq, D_MODEL)
    for layer in range(w_in.shape[0]):
        w = w_in[layer]
        small = jnp.pad(w[:, o_ki:o_gc], ((0, 0), (0, LANES - (o_gc - o_ki))))
        w_attn = jnp.concatenate([w[:, o_q:o_ki], small], axis=1).astype(BF16)
        w_ab = w[:, o_a:o_q].astype(BF16)
        w_g = w[:, o_gc:o_ga + D_MODEL].astype(BF16)

        qt, k, vt, qit, ki, wit = _inproj(
            h, w_attn, cos_t, sin_t,
            _pad_lanes(idx_k_ln_g[layer], LANES), _pad_lanes(idx_k_ln_b[layer], LANES),
            batch=batch, seq=seq, tm=tk)
        attn = _dsa(qit, wit, ki, qt, k, vt, batch=batch, seq=seq, tq=tq, tk=tk)

        dww = jnp.pad(dw_w[layer], ((0, CONV_HALO - CONV_WIDTH), (0, 0)))
        conv_feat = _conv_branch(
            h, w_ab, dww, dw_b[layer].reshape(1, D_CONV),
            conv_ln_g[layer].reshape(1, D_CONV), conv_ln_b[layer].reshape(1, D_CONV),
            batch=batch, seq=seq, tm=tm)

        h = _mixer(h, conv_feat, attn, w_g, gate_b[layer].reshape(1, 2 * D_MODEL),
                   w_conv_out[layer].astype(BF16), w_attn_out[layer].astype(BF16),
                   w_out[layer].astype(BF16),
                   ln1_g[layer].reshape(1, D_MODEL), ln1_b[layer].reshape(1, D_MODEL), tm=tm)
        h = _ffn(h, w_ff_in[layer].astype(BF16), w_ff_out[layer].astype(BF16),
                 ln2_g[layer].reshape(1, D_MODEL), ln2_b[layer].reshape(1, D_MODEL),
                 tm=tm, ff_chunk=1024)
    return h.reshape(batch, seq, D_MODEL)
```

```python
import functools

import jax
import jax.numpy as jnp
from jax import lax
from jax.experimental import pallas as pl
from jax.experimental.pallas import tpu as pltpu

D_MODEL = 1024
CHUNK = 64
D_CONV = 512
CONV_WIDTH = 31
N_HEADS = 8
HEAD_DIM = 64
D_ATTN = N_HEADS * HEAD_DIM
N_IDX_HEADS = 8
IDX_DIM = 64
TOPK_MAX = 256
D_FF = 4 * D_MODEL
ROPE_THETA = 10000.0
LN_EPS = 1e-5
NEG_INF = -1e30
DEPTH = 1
DEEPNORM_ALPHA = (2.0 * DEPTH) ** 0.25

LANES = 128
SUBLANES = 8
CONV_HALO = 32
ONES_ROWS = 16
VT_ROWS = HEAD_DIM + ONES_ROWS
ACC_ROWS = 4 * SUBLANES
BISECT_PASSES = 26
SOFTMAX_ROWS = 128
LOG2_E = 1.4426950408889634
INT_MIN = -(2 ** 31)
KEY_NEG_FLT_MAX = -(2 ** 31) + 0x00800000
M_FLOOR = -1e20
VMEM_LIMIT = 56 * 1024 * 1024

F32 = jnp.float32
BF16 = jnp.bfloat16


def _dot(a, b):
    return jnp.dot(a, b, preferred_element_type=F32)


def _layer_norm(x, g, b):
    mu = jnp.mean(x, axis=-1, keepdims=True)
    d = x - mu
    var = jnp.mean(d * d, axis=-1, keepdims=True)
    return d * lax.rsqrt(var + LN_EPS) * g + b


def _rope(p, cos, sin_signed):
    width = p.shape[-1]
    lane = lax.broadcasted_iota(jnp.int32, p.shape, 1)
    first_half = (lane % HEAD_DIM) < (HEAD_DIM // 2)
    swapped = jnp.where(first_half,
                        pltpu.roll(p, width - HEAD_DIM // 2, 1),
                        pltpu.roll(p, HEAD_DIM // 2, 1))
    return p * cos + swapped * sin_signed


def _rope_transposed(pt, cos_t, sin_signed_t):
    half = HEAD_DIM // 2
    blocks = []
    for r0 in range(0, pt.shape[0], HEAD_DIM):
        blocks += [pt[r0 + half:r0 + HEAD_DIM], pt[r0:r0 + half]]
    return pt * cos_t + jnp.concatenate(blocks, axis=0) * sin_signed_t


def _inproj_kernel(x_ref, w_ref, cos_ref, sin_ref, cost_ref, sint_ref, kig_ref, kib_ref,
                   qt_ref, k_ref, vt_ref, qit_ref, ki_ref, wit_ref):
    xb = x_ref[...].astype(BF16)
    tm = xb.shape[0]
    cos = cos_ref[...]
    sin = sin_ref[...]
    cos_t = cost_ref[...]
    sin_t = sint_ref[...]
    qt = _rope_transposed(_dot(xb, w_ref[:, 0:D_ATTN]).T, cos_t, sin_t)
    qt_ref[...] = (qt * (HEAD_DIM ** -0.5 * LOG2_E)).astype(BF16)
    k_ref[...] = _rope(_dot(xb, w_ref[:, D_ATTN:2 * D_ATTN]), cos, sin).astype(BF16)

    vt = _dot(xb, w_ref[:, 2 * D_ATTN:3 * D_ATTN]).T
    ones = jnp.ones((ONES_ROWS, tm), F32)
    pieces = []
    for h in range(N_HEADS):
        pieces += [vt[h * HEAD_DIM:(h + 1) * HEAD_DIM], ones]
    vt_ref[...] = jnp.concatenate(pieces, axis=0).astype(BF16)

    qit = _rope_transposed(_dot(xb, w_ref[:, 3 * D_ATTN:4 * D_ATTN]).T, cos_t, sin_t)
    qit_ref[...] = (qit * (IDX_DIM ** -0.5)).astype(BF16)

    sm = _dot(xb, w_ref[:, 4 * D_ATTN:4 * D_ATTN + LANES])
    lane = lax.broadcasted_iota(jnp.int32, sm.shape, 1)
    is_ki = lane < IDX_DIM
    mu = jnp.sum(jnp.where(is_ki, sm, 0.0), axis=-1, keepdims=True) * (1.0 / IDX_DIM)
    d = jnp.where(is_ki, sm - mu, 0.0)
    var = jnp.sum(d * d, axis=-1, keepdims=True) * (1.0 / IDX_DIM)
    kn = d * lax.rsqrt(var + LN_EPS) * kig_ref[...] + kib_ref[...]
    kr = _rope(kn, cos[:, :LANES], sin[:, :LANES])
    ki_ref[...] = kr[:, :IDX_DIM].astype(BF16)
    wit_ref[...] = sm.T[IDX_DIM:IDX_DIM + N_IDX_HEADS] * (N_IDX_HEADS ** -0.5)


def _inproj(x2, w_attn, cos_t, sin_t, kig, kib, *, batch, seq, tm):
    n = batch * seq
    nt = seq // tm
    row = lambda j, b: (b * nt + j, 0)
    col = lambda j, b: (b, j)
    pos = lambda j, b: (j, 0)
    const = lambda j, b: (0, 0)
    wcols = w_attn.shape[1]
    return pl.pallas_call(
        _inproj_kernel,
        grid=(nt, batch),
        in_specs=[
            pl.BlockSpec((tm, D_MODEL), row),
            pl.BlockSpec((D_MODEL, wcols), const),
            pl.BlockSpec((tm, D_ATTN), pos),
            pl.BlockSpec((tm, D_ATTN), pos),
            pl.BlockSpec((D_ATTN, tm), lambda j, b: (0, j)),
            pl.BlockSpec((D_ATTN, tm), lambda j, b: (0, j)),
            pl.BlockSpec((1, LANES), const),
            pl.BlockSpec((1, LANES), const),
        ],
        out_specs=[
            pl.BlockSpec((D_ATTN, tm), col),
            pl.BlockSpec((tm, D_ATTN), row),
            pl.BlockSpec((None, None, N_HEADS * VT_ROWS, tm), lambda j, b: (b, j, 0, 0)),
            pl.BlockSpec((D_ATTN, tm), col),
            pl.BlockSpec((tm, IDX_DIM), row),
            pl.BlockSpec((N_IDX_HEADS, tm), col),
        ],
        out_shape=[
            jax.ShapeDtypeStruct((batch * D_ATTN, seq), BF16),
            jax.ShapeDtypeStruct((n, D_ATTN), BF16),
            jax.ShapeDtypeStruct((batch, nt, N_HEADS * VT_ROWS, tm), BF16),
            jax.ShapeDtypeStruct((batch * D_ATTN, seq), BF16),
            jax.ShapeDtypeStruct((n, IDX_DIM), BF16),
            jax.ShapeDtypeStruct((batch * N_IDX_HEADS, seq), F32),
        ],
        compiler_params=pltpu.CompilerParams(
            dimension_semantics=("arbitrary", "arbitrary"), vmem_limit_bytes=VMEM_LIMIT),
        name="inproj",
    )(x2, w_attn, cos_t, sin_t, cos_t.T, sin_t.T, kig, kib)


def _key_to_f32(key):
    bits = key ^ ((key >> 31) & jnp.int32(0x7FFFFFFF))
    return lax.bitcast_convert_type(bits, F32)


def _fold_rows(x, op):
    rows, cols = x.shape
    return op(x.reshape(rows // ACC_ROWS, ACC_ROWS, cols), axis=0)


def _dsa_kernel(qit_ref, wit_ref, ki_ref, qt_ref, k_ref, vt_ref, o_ref,
                isc_s, qz_s, thr_s, need_s, m_s, mblk_s, alpha_s, acc_s, s_st, p_st, *, tq, tk, topk, seq):
    j = pl.program_id(1)
    t0 = j * tq
    n_kb = (t0 + tq + tk - 1) // tk

    chunk_shift = CHUNK.bit_length() - 1
    q_chunk = (t0 + lax.broadcasted_iota(jnp.int32, (tk, tq), 1)) >> chunk_shift
    key_off = lax.broadcasted_iota(jnp.int32, (tk, tq), 0)

    def score_block(kb, carry, on_diagonal):
        hi_part, lo_part, pos_part, nonneg_part = carry
        kib = ki_ref[pl.ds(pl.multiple_of(kb * tk, tk), tk), :]
        isc = jnp.zeros((tk, tq), F32)
        for h in range(N_IDX_HEADS):
            lg = _dot(kib, qit_ref[h * IDX_DIM:(h + 1) * IDX_DIM, :])
            isc = isc + jnp.maximum(lg, 0.0) * wit_ref[h:h + 1, :]
        if on_diagonal:
            admissible = ((kb * tk + key_off) >> chunk_shift) <= q_chunk
            masked = jnp.where(admissible, isc, -jnp.inf)
        else:
            masked = isc
        isc_s[kb] = masked
        return (jnp.maximum(hi_part, _fold_rows(isc, jnp.max)),
                jnp.minimum(lo_part, _fold_rows(isc, jnp.min)),
                pos_part + _fold_rows(jnp.where(masked > 0.0, 1, 0), jnp.sum),
                nonneg_part + _fold_rows(jnp.where(masked >= 0.0, 1, 0), jnp.sum))

    carry = lax.fori_loop(
        0, n_kb - 1, functools.partial(score_block, on_diagonal=False),
        (jnp.full((ACC_ROWS, tq), -jnp.inf, F32), jnp.full((ACC_ROWS, tq), jnp.inf, F32),
         jnp.zeros((ACC_ROWS, tq), jnp.int32), jnp.zeros((ACC_ROWS, tq), jnp.int32)))
    hi_part, lo_part, pos_part, nonneg_part = score_block(n_kb - 1, carry, on_diagonal=True)
    s_max = jnp.max(hi_part, axis=0, keepdims=True)
    s_min = jnp.min(lo_part, axis=0, keepdims=True)
    n_pos = jnp.sum(pos_part, axis=0, keepdims=True)
    n_nonneg = jnp.sum(nonneg_part, axis=0, keepdims=True)

    def count_keys(pred_fn):
        def body(kb, part):
            hit = jnp.where(pred_fn(kb, isc_s[kb]), 1, 0)
            return part + _fold_rows(hit, jnp.sum)
        part = lax.fori_loop(0, n_kb, body, jnp.zeros((ACC_ROWS, tq), jnp.int32))
        return jnp.sum(part, axis=0, keepdims=True)

    q_pos = t0 + lax.broadcasted_iota(jnp.int32, (1, tq), 1)
    n_admissible = ((q_pos >> chunk_shift) + 1) << chunk_shift
    keep_all = n_admissible <= topk
    zero_thr = jnp.logical_and(jnp.logical_not(keep_all),
                               jnp.logical_and(n_pos < topk, n_nonneg >= topk))
    pinned = jnp.logical_or(keep_all, zero_thr)
    positive = n_pos >= topk
    pin = jnp.where(keep_all, s_min, 0.0)
    lo = jnp.where(pinned, pin, jnp.where(positive, 0.0, s_min))
    hi = jnp.where(pinned, pin, jnp.where(positive, s_max, 0.0))
    n_lo = jnp.where(jnp.logical_or(keep_all, jnp.logical_not(jnp.logical_or(zero_thr, positive))),
                     n_admissible, n_nonneg)

    def bisect_step(_, carry):
        lo, hi, n_lo = carry
        mid = lo + (hi - lo) * 0.5
        cnt = count_keys(lambda kb, blk: blk >= mid)
        keep = cnt >= topk
        return jnp.where(keep, mid, lo), jnp.where(keep, hi, mid), jnp.where(keep, cnt, n_lo)

    lo, _, n_lo = lax.fori_loop(0, BISECT_PASSES, bisect_step, (lo, hi, n_lo))
    thr_s[...] = jnp.broadcast_to(lo, thr_s.shape)
    unresolved = jnp.max(jnp.where(jnp.logical_or(zero_thr, n_lo <= topk), 0, 1)) > 0
    zero_ties = jnp.logical_and(zero_thr, n_nonneg > topk)
    need_s[...] = jnp.broadcast_to(jnp.where(zero_ties, topk - n_pos, seq), need_s.shape)
    tied = jnp.max(jnp.where(zero_ties, 1, 0)) > 0

    def to_bias(select_fn):
        def body(kb, carry):
            isc_s[kb] = jnp.where(select_fn(kb, isc_s[kb]), 0.0, NEG_INF)
            return carry
        lax.fori_loop(0, n_kb, body, 0)

    def to_bias_ranked(thr_v, need):
        lower_tri = jnp.where(lax.broadcasted_iota(jnp.int32, (tk, tk), 0)
                              >= lax.broadcasted_iota(jnp.int32, (tk, tk), 1), 1.0, 0.0).astype(BF16)
        need_f = need.astype(F32)

        def body(kb, before):
            blk = isc_s[kb]
            tie = blk == thr_v
            rank = before + _dot(lower_tri, jnp.where(tie, 1.0, 0.0).astype(BF16))
            keep = jnp.logical_or(blk > thr_v, jnp.logical_and(tie, rank <= need_f))
            isc_s[kb] = jnp.where(keep, 0.0, NEG_INF)
            return rank[tk - 1:tk, :]
        lax.fori_loop(0, n_kb, body, jnp.zeros((1, tq), F32))

    @pl.when(jnp.logical_not(jnp.logical_or(unresolved, tied)))
    def _():
        thr_v = thr_s[0:1, :]
        to_bias(lambda kb, blk: blk >= thr_v)

    @pl.when(jnp.logical_and(jnp.logical_not(unresolved), tied))
    def _():
        to_bias_ranked(thr_s[0:1, :], need_s[0:1, :])

    @pl.when(unresolved)
    def _():
        def bit_step(i, carry):
            thr_key, n_ge = carry
            cand = jnp.where(i == 0, jnp.zeros_like(thr_key), thr_key | (jnp.int32(1) << (31 - i)))
            cand_f = _key_to_f32(jnp.maximum(cand, KEY_NEG_FLT_MAX))
            cnt = count_keys(lambda kb, blk: blk >= cand_f)
            accept = cnt >= topk
            return jnp.where(accept, cand, thr_key), jnp.where(accept, cnt, n_ge)

        thr_key, n_ge = lax.fori_loop(
            0, 32, bit_step, (jnp.full((1, tq), INT_MIN, jnp.int32), jnp.zeros((1, tq), jnp.int32)))
        thr_v = _key_to_f32(jnp.maximum(thr_key, KEY_NEG_FLT_MAX))
        has_ties = jnp.max(n_ge) > topk

        @pl.when(jnp.logical_not(has_ties))
        def _():
            to_bias(lambda kb, blk: blk >= thr_v)

        @pl.when(has_ties)
        def _():
            n_gt = count_keys(lambda kb, blk: blk > thr_v)
            to_bias_ranked(thr_v, jnp.where(n_ge > topk, topk - n_gt, seq))

    zeros_half = jnp.zeros((HEAD_DIM, tq), BF16)
    for h in range(N_HEADS):
        head_rows = qt_ref[h * HEAD_DIM:(h + 1) * HEAD_DIM, :]
        pair = [head_rows, zeros_half] if h % 2 == 0 else [zeros_half, head_rows]
        qz_s[h] = jnp.concatenate(pair, axis=0)
    m_s[...] = jnp.full(m_s.shape, M_FLOOR, F32)
    acc_s[...] = jnp.zeros(acc_s.shape, F32)

    chunks = [slice(c, c + SOFTMAX_ROWS) for c in range(0, tk, SOFTMAX_ROWS)]

    def scores(kb, h):
        rows = pl.ds(pl.multiple_of(kb * tk, tk), tk)
        slab = slice((h // 2) * LANES, (h // 2 + 1) * LANES)
        s = _dot(k_ref[rows, slab], qz_s[h]) + isc_s[kb]
        s_st[h] = s
        mblk_s[h] = jnp.broadcast_to(
            jnp.max(_fold_rows(s, jnp.max), axis=0, keepdims=True), (SUBLANES, tq))

    def probs(h):
        m_old = m_s[h]
        m_new = jnp.maximum(m_old, mblk_s[h])
        alpha_s[h] = jnp.exp2(m_old - m_new)
        for c in chunks:
            p_st[h, c, :] = jnp.exp2(s_st[h, c, :] - m_new[0:1, :]).astype(BF16)
        m_s[h] = m_new

    def accum(kb, h):
        pv = _dot(vt_ref[kb, h * VT_ROWS:(h + 1) * VT_ROWS, :], p_st[h])
        acc_s[h] = alpha_s[h][0:1, :] * acc_s[h] + pv

    for h in range(N_HEADS):
        scores(0, h)

    @pl.when(n_kb < 2)
    def _():
        for h in range(N_HEADS):
            probs(h)

    @pl.when(n_kb >= 2)
    def _():
        for h in range(N_HEADS):
            probs(h)
        for h in range(N_HEADS):
            scores(1, h)

    def steady(t, carry):
        for h in range(N_HEADS):
            accum(t - 2, h)
        for h in range(N_HEADS):
            probs(h)
        for h in range(N_HEADS):
            scores(t, h)
        return carry

    lax.fori_loop(2, n_kb, steady, 0)

    @pl.when(n_kb >= 2)
    def _():
        for h in range(N_HEADS):
            accum(n_kb - 2, h)
        for h in range(N_HEADS):
            probs(h)

    for h in range(N_HEADS):
        accum(n_kb - 1, h)

    outs = []
    for h in range(N_HEADS):
        a = acc_s[h]
        outs.append(a[:HEAD_DIM] / a[HEAD_DIM:HEAD_DIM + 1])
    o_ref[...] = jnp.concatenate(outs, axis=0).T.astype(BF16)


def _dsa(qit, wit, ki, qt, k, vt, *, batch, seq, tq, tk):
    assert tk % tq == 0 and tq % CHUNK == 0 and seq % tk == 0, (seq, tq, tk)
    n = batch * seq
    nt = seq // tq
    topk = min(TOPK_MAX, seq // 4)
    n_blk = seq // tk
    qcol = lambda b, j: (b, j)
    whole = lambda b, j: (b, 0)
    kern = functools.partial(_dsa_kernel, tq=tq, tk=tk, topk=topk, seq=seq)
    return pl.pallas_call(
        kern,
        grid=(batch, nt),
        in_specs=[
            pl.BlockSpec((D_ATTN, tq), qcol),
            pl.BlockSpec((N_IDX_HEADS, tq), qcol),
            pl.BlockSpec((seq, IDX_DIM), whole),
            pl.BlockSpec((D_ATTN, tq), qcol),
            pl.BlockSpec((seq, D_ATTN), whole),
            pl.BlockSpec((None, n_blk, N_HEADS * VT_ROWS, tk), lambda b, j: (b, 0, 0, 0)),
        ],
        out_specs=pl.BlockSpec((tq, D_ATTN), lambda b, j: (b * nt + j, 0)),
        out_shape=jax.ShapeDtypeStruct((n, D_ATTN), BF16),
        scratch_shapes=[
            pltpu.VMEM((n_blk, tk, tq), F32),
            pltpu.VMEM((N_HEADS, LANES, tq), BF16),
            pltpu.VMEM((SUBLANES, tq), F32),
            pltpu.VMEM((SUBLANES, tq), jnp.int32),
            pltpu.VMEM((N_HEADS, SUBLANES, tq), F32),
            pltpu.VMEM((N_HEADS, SUBLANES, tq), F32),
            pltpu.VMEM((N_HEADS, SUBLANES, tq), F32),
            pltpu.VMEM((N_HEADS, VT_ROWS, tq), F32),
            pltpu.VMEM((N_HEADS, tk, tq), F32),
            pltpu.VMEM((N_HEADS, tk, tq), BF16),
        ],
        compiler_params=pltpu.CompilerParams(
            dimension_semantics=("arbitrary", "arbitrary"), vmem_limit_bytes=VMEM_LIMIT),
        name="dsa",
    )(qit, wit, ki, qt, k, vt)


def _conv_kernel(x_ref, xh_ref, wab_ref, dww_ref, dwb_ref, g_ref, b_ref, o_ref, u_s, ush_s, c_s, *, tm):
    j = pl.program_id(1)

    def glu(xb):
        ab = _dot(xb, wab_ref[...])
        return ab[:, :D_CONV] * jax.nn.sigmoid(ab[:, D_CONV:])

    u_halo = glu(xh_ref[...].astype(BF16))
    u_s[0:CONV_HALO, :] = jnp.where(j > 0, u_halo, 0.0)
    u_s[CONV_HALO:CONV_HALO + tm, :] = glu(x_ref[...].astype(BF16))

    shifted_rows = CONV_HALO + tm - SUBLANES
    for r in range(1, SUBLANES):
        ush_s[r - 1, 0:shifted_rows, :] = u_s[r:r + shifted_rows, :]

    rows_per_step = 64
    first = CONV_HALO - (CONV_WIDTH - 1)
    for r0 in range(0, tm, rows_per_step):
        for c0 in range(0, D_CONV, LANES):
            acc = jnp.broadcast_to(dwb_ref[:, c0:c0 + LANES], (rows_per_step, LANES))
            for tap in range(CONV_WIDTH):
                phase = (first + tap) % SUBLANES
                base = r0 + (first + tap) - phase
                src = u_s if phase == 0 else ush_s.at[phase - 1]
                acc = acc + (src[base:base + rows_per_step, c0:c0 + LANES]
                             * dww_ref[tap:tap + 1, c0:c0 + LANES])
            c_s[r0:r0 + rows_per_step, c0:c0 + LANES] = acc

    y = _layer_norm(c_s[...], g_ref[...], b_ref[...])
    o_ref[...] = (y * jax.nn.sigmoid(y)).astype(BF16)


def _conv_branch(x2, w_ab, dw_w, dw_b, ln_g, ln_b, *, batch, seq, tm):
    n = batch * seq
    nt = seq // tm
    halo_per_tile = tm // CONV_HALO
    halo_per_seq = seq // CONV_HALO
    row = lambda b, j: (b * nt + j, 0)
    halo = lambda b, j: (jnp.maximum(b * halo_per_seq + j * halo_per_tile - 1, 0), 0)
    const = lambda b, j: (0, 0)
    return pl.pallas_call(
        functools.partial(_conv_kernel, tm=tm),
        grid=(batch, nt),
        in_specs=[
            pl.BlockSpec((tm, D_MODEL), row),
            pl.BlockSpec((CONV_HALO, D_MODEL), halo),
            pl.BlockSpec((D_MODEL, 2 * D_CONV), const),
            pl.BlockSpec((CONV_HALO, D_CONV), const),
            pl.BlockSpec((1, D_CONV), const),
            pl.BlockSpec((1, D_CONV), const),
            pl.BlockSpec((1, D_CONV), const),
        ],
        out_specs=pl.BlockSpec((tm, D_CONV), row),
        out_shape=jax.ShapeDtypeStruct((n, D_CONV), BF16),
        scratch_shapes=[
            pltpu.VMEM((CONV_HALO + tm, D_CONV), F32),
            pltpu.VMEM((SUBLANES - 1, CONV_HALO + tm, D_CONV), F32),
            pltpu.VMEM((tm, D_CONV), F32),
        ],
        compiler_params=pltpu.CompilerParams(
            dimension_semantics=("arbitrary", "arbitrary"), vmem_limit_bytes=VMEM_LIMIT),
        name="conv_branch",
    )(x2, x2, w_ab, dw_w, dw_b, ln_g, ln_b)


def _mixer_kernel(x_ref, cf_ref, at_ref, wg_ref, gb_ref, wco_ref, wao_ref, wout_ref, g_ref, b_ref, o_ref):
    x = x_ref[...]
    xb = x.astype(BF16)
    gate_c = jax.nn.sigmoid(_dot(xb, wg_ref[:, :D_MODEL]) + gb_ref[:, :D_MODEL])
    merged = gate_c * _dot(cf_ref[...], wco_ref[...])
    gate_a = jax.nn.sigmoid(_dot(xb, wg_ref[:, D_MODEL:]) + gb_ref[:, D_MODEL:])
    merged = merged + gate_a * _dot(at_ref[...], wao_ref[...])
    mixer = _dot(merged.astype(BF16), wout_ref[...])
    o_ref[...] = _layer_norm(DEEPNORM_ALPHA * x + mixer, g_ref[...], b_ref[...])


def _mixer(x2, cf, at, w_g, gate_b, w_co, w_ao, w_out, ln_g, ln_b, *, tm):
    n = x2.shape[0]
    row = lambda i: (i, 0)
    const = lambda i: (0, 0)
    return pl.pallas_call(
        _mixer_kernel,
        grid=(n // tm,),
        in_specs=[
            pl.BlockSpec((tm, D_MODEL), row),
            pl.BlockSpec((tm, D_CONV), row),
            pl.BlockSpec((tm, D_ATTN), row),
            pl.BlockSpec((D_MODEL, 2 * D_MODEL), const),
            pl.BlockSpec((1, 2 * D_MODEL), const),
            pl.BlockSpec((D_CONV, D_MODEL), const),
            pl.BlockSpec((D_ATTN, D_MODEL), const),
            pl.BlockSpec((D_MODEL, D_MODEL), const),
            pl.BlockSpec((1, D_MODEL), const),
            pl.BlockSpec((1, D_MODEL), const),
        ],
        out_specs=pl.BlockSpec((tm, D_MODEL), row),
        out_shape=jax.ShapeDtypeStruct((n, D_MODEL), F32),
        compiler_params=pltpu.CompilerParams(
            dimension_semantics=("arbitrary",), vmem_limit_bytes=VMEM_LIMIT),
        name="mixer",
    )(x2, cf, at, w_g, gate_b, w_co, w_ao, w_out, ln_g, ln_b)


def _ffn_kernel(x_ref, wi_ref, wo_ref, g_ref, b_ref, o_ref, *, ff_chunk):
    x = x_ref[...]
    xb = x.astype(BF16)
    acc = jnp.zeros(x.shape, F32)
    for c0 in range(0, D_FF, ff_chunk):
        h = jnp.maximum(_dot(xb, wi_ref[:, c0:c0 + ff_chunk]), 0.0)
        acc = acc + _dot((h * h).astype(BF16), wo_ref[c0:c0 + ff_chunk, :])
    o_ref[...] = _layer_norm(DEEPNORM_ALPHA * x + acc, g_ref[...], b_ref[...])


def _ffn(x2, w_i, w_o, ln_g, ln_b, *, tm, ff_chunk):
    n = x2.shape[0]
    row = lambda i: (i, 0)
    const = lambda i: (0, 0)
    return pl.pallas_call(
        functools.partial(_ffn_kernel, ff_chunk=ff_chunk),
        grid=(n // tm,),
        in_specs=[
            pl.BlockSpec((tm, D_MODEL), row),
            pl.BlockSpec((D_MODEL, D_FF), const, pipeline_mode=pl.Buffered(1)),
            pl.BlockSpec((D_FF, D_MODEL), const, pipeline_mode=pl.Buffered(1)),
            pl.BlockSpec((1, D_MODEL), const),
            pl.BlockSpec((1, D_MODEL), const),
        ],
        out_specs=pl.BlockSpec((tm, D_MODEL), row),
        out_shape=jax.ShapeDtypeStruct((n, D_MODEL), F32),
        compiler_params=pltpu.CompilerParams(
            dimension_semantics=("arbitrary",), vmem_limit_bytes=VMEM_LIMIT),
        name="ffn",
    )(x2, w_i, w_o, ln_g, ln_b)


def _rope_tables(seq):
    inv_freq = ROPE_THETA ** (-jnp.arange(0, HEAD_DIM, 2, dtype=F32) / HEAD_DIM)
    ang = jnp.arange(seq, dtype=jnp.int32).astype(F32)[:, None] * inv_freq[None, :]
    cos = jnp.cos(ang)
    sin = jnp.sin(ang)
    cos_blk = jnp.concatenate([cos, cos], axis=1)
    sin_blk = jnp.concatenate([-sin, sin], axis=1)
    return jnp.tile(cos_blk, (1, N_HEADS)), jnp.tile(sin_blk, (1, N_HEADS))


def _pad_lanes(v, width):
    return jnp.pad(v, (0, width - v.shape[0])).reshape(1, width)


def kernel(x, w_in, dw_w, dw_b, conv_ln_g, conv_ln_b, w_conv_out, idx_k_ln_g, idx_k_ln_b,
           w_attn_out, gate_b, w_out, ln1_g, ln1_b, w_ff_in, w_ff_out, ln2_g, ln2_b):
    batch, seq, _ = x.shape
    cos_t, sin_t = _rope_tables(seq)
    tm = min(512, seq)
    tq = min(256, seq)
    tk = min(512, seq)

    o_a, o_b, o_q, o_k, o_v = 0, D_CONV, 2 * D_CONV, 2 * D_CONV + D_ATTN, 2 * D_CONV + 2 * D_ATTN
    o_qi = o_v + D_ATTN
    o_ki = o_qi + N_IDX_HEADS * IDX_DIM
    o_wi = o_ki + IDX_DIM
    o_gc = o_wi + N_IDX_HEADS
    o_ga = o_gc + D_MODEL

    h = x.reshape(batch * seq, D_MODEL)
    for layer in range(w_in.shape[0]):
        w = w_in[layer]
        small = jnp.pad(w[:, o_ki:o_gc], ((0, 0), (0, LANES - (o_gc - o_ki))))
        w_attn = jnp.concatenate([w[:, o_q:o_ki], small], axis=1).astype(BF16)
        w_ab = w[:, o_a:o_q].astype(BF16)
        w_g = w[:, o_gc:o_ga + D_MODEL].astype(BF16)

        qt, k, vt, qit, ki, wit = _inproj(
            h, w_attn, cos_t, sin_t,
            _pad_lanes(idx_k_ln_g[layer], LANES), _pad_lanes(idx_k_ln_b[layer], LANES),
            batch=batch, seq=seq, tm=tk)
        attn = _dsa(qit, wit, ki, qt, k, vt, batch=batch, seq=seq, tq=tq, tk=tk)

        dww = jnp.pad(dw_w[layer], ((0, CONV_HALO - CONV_WIDTH), (0, 0)))
        conv_feat = _conv_branch(
            h, w_ab, dww, dw_b[layer].reshape(1, D_CONV),
            conv_ln_g[layer].reshape(1, D_CONV), conv_ln_b[layer].reshape(1, D_CONV),
            batch=batch, seq=seq, tm=tm)

        h = _mixer(h, conv_feat, attn, w_g, gate_b[layer].reshape(1, 2 * D_MODEL),
                   w_conv_out[layer].astype(BF16), w_attn_out[layer].astype(BF16),
                   w_out[layer].astype(BF16),
                   ln1_g[layer].reshape(1, D_MODEL), ln1_b[layer].reshape(1, D_MODEL), tm=tm)
        h = _ffn(h, w_ff_in[layer].astype(BF16), w_ff_out[layer].astype(BF16),
                 ln2_g[layer].reshape(1, D_MODEL), ln2_b[layer].reshape(1, D_MODEL),
                 tm=tm, ff_chunk=1024)
    return h.reshape(batch, seq, D_MODEL)
```

```python
import functools

import jax
import jax.numpy as jnp
from jax import lax
from jax.experimental import pallas as pl
from jax.experimental.pallas import tpu as pltpu

D_MODEL = 1024
CHUNK = 64
D_CONV = 512
CONV_WIDTH = 31
N_HEADS = 8
HEAD_DIM = 64
D_ATTN = N_HEADS * HEAD_DIM
N_IDX_HEADS = 8
IDX_DIM = 64
TOPK_MAX = 256
D_FF = 4 * D_MODEL
ROPE_THETA = 10000.0
LN_EPS = 1e-5
NEG_INF = -1e30
DEPTH = 1
DEEPNORM_ALPHA = (2.0 * DEPTH) ** 0.25

LANES = 128
SUBLANES = 8
CONV_HALO = 32
ONES_ROWS = 16
VT_ROWS = HEAD_DIM + ONES_ROWS
ACC_ROWS = 4 * SUBLANES
BISECT_PASSES = 26
SOFTMAX_ROWS = 128
LOG2_E = 1.4426950408889634
INT_MIN = -(2 ** 31)
KEY_NEG_FLT_MAX = -(2 ** 31) + 0x00800000
M_FLOOR = -1e20
VMEM_LIMIT = 56 * 1024 * 1024

F32 = jnp.float32
BF16 = jnp.bfloat16


def _dot(a, b):
    return jnp.dot(a, b, preferred_element_type=F32)


def _layer_norm(x, g, b):
    mu = jnp.mean(x, axis=-1, keepdims=True)
    d = x - mu
    var = jnp.mean(d * d, axis=-1, keepdims=True)
    return d * lax.rsqrt(var + LN_EPS) * g + b


def _rope(p, cos, sin_signed):
    width = p.shape[-1]
    lane = lax.broadcasted_iota(jnp.int32, p.shape, 1)
    first_half = (lane % HEAD_DIM) < (HEAD_DIM // 2)
    swapped = jnp.where(first_half,
                        pltpu.roll(p, width - HEAD_DIM // 2, 1),
                        pltpu.roll(p, HEAD_DIM // 2, 1))
    return p * cos + swapped * sin_signed


def _rope_transposed(pt, cos_t, sin_signed_t):
    half = HEAD_DIM // 2
    blocks = []
    for r0 in range(0, pt.shape[0], HEAD_DIM):
        blocks += [pt[r0 + half:r0 + HEAD_DIM], pt[r0:r0 + half]]
    return pt * cos_t + jnp.concatenate(blocks, axis=0) * sin_signed_t


def _inproj_kernel(x_ref, w_ref, cos_ref, sin_ref, cost_ref, sint_ref, kig_ref, kib_ref,
                   qt_ref, k_ref, vt_ref, qit_ref, ki_ref, wit_ref):
    xb = x_ref[...].astype(BF16)
    tm = xb.shape[0]
    cos = cos_ref[...]
    sin = sin_ref[...]
    cos_t = cost_ref[...]
    sin_t = sint_ref[...]
    qt = _rope_transposed(_dot(xb, w_ref[:, 0:D_ATTN]).T, cos_t, sin_t)
    qt_ref[...] = (qt * (HEAD_DIM ** -0.5 * LOG2_E)).astype(BF16)
    k_ref[...] = _rope(_dot(xb, w_ref[:, D_ATTN:2 * D_ATTN]), cos, sin).astype(BF16)

    vt = _dot(xb, w_ref[:, 2 * D_ATTN:3 * D_ATTN]).T
    ones = jnp.ones((ONES_ROWS, tm), F32)
    pieces = []
    for h in range(N_HEADS):
        pieces += [vt[h * HEAD_DIM:(h + 1) * HEAD_DIM], ones]
    vt_ref[...] = jnp.concatenate(pieces, axis=0).astype(BF16)

    qit = _rope_transposed(_dot(xb, w_ref[:, 3 * D_ATTN:4 * D_ATTN]).T, cos_t, sin_t)
    qit_ref[...] = (qit * (IDX_DIM ** -0.5)).astype(BF16)

    sm = _dot(xb, w_ref[:, 4 * D_ATTN:4 * D_ATTN + LANES])
    lane = lax.broadcasted_iota(jnp.int32, sm.shape, 1)
    is_ki = lane < IDX_DIM
    mu = jnp.sum(jnp.where(is_ki, sm, 0.0), axis=-1, keepdims=True) * (1.0 / IDX_DIM)
    d = jnp.where(is_ki, sm - mu, 0.0)
    var = jnp.sum(d * d, axis=-1, keepdims=True) * (1.0 / IDX_DIM)
    kn = d * lax.rsqrt(var + LN_EPS) * kig_ref[...] + kib_ref[...]
    kr = _rope(kn, cos[:, :LANES], sin[:, :LANES])
    ki_ref[...] = kr[:, :IDX_DIM].astype(BF16)
    wit_ref[...] = sm.T[IDX_DIM:IDX_DIM + N_IDX_HEADS] * (N_IDX_HEADS ** -0.5)


def _inproj(x2, w_attn, cos_t, sin_t, kig, kib, *, batch, seq, tm):
    n = batch * seq
    nt = seq // tm
    row = lambda j, b: (b * nt + j, 0)
    col = lambda j, b: (b, j)
    pos = lambda j, b: (j, 0)
    const = lambda j, b: (0, 0)
    wcols = w_attn.shape[1]
    return pl.pallas_call(
        _inproj_kernel,
        grid=(nt, batch),
        in_specs=[
            pl.BlockSpec((tm, D_MODEL), row),
            pl.BlockSpec((D_MODEL, wcols), const),
            pl.BlockSpec((tm, D_ATTN), pos),
            pl.BlockSpec((tm, D_ATTN), pos),
            pl.BlockSpec((D_ATTN, tm), lambda j, b: (0, j)),
            pl.BlockSpec((D_ATTN, tm), lambda j, b: (0, j)),
            pl.BlockSpec((1, LANES), const),
            pl.BlockSpec((1, LANES), const),
        ],
        out_specs=[
            pl.BlockSpec((D_ATTN, tm), col),
            pl.BlockSpec((tm, D_ATTN), row),
            pl.BlockSpec((None, None, N_HEADS * VT_ROWS, tm), lambda j, b: (b, j, 0, 0)),
            pl.BlockSpec((D_ATTN, tm), col),
            pl.BlockSpec((tm, IDX_DIM), row),
            pl.BlockSpec((N_IDX_HEADS, tm), col),
        ],
        out_shape=[
            jax.ShapeDtypeStruct((batch * D_ATTN, seq), BF16),
            jax.ShapeDtypeStruct((n, D_ATTN), BF16),
            jax.ShapeDtypeStruct((batch, nt, N_HEADS * VT_ROWS, tm), BF16),
            jax.ShapeDtypeStruct((batch * D_ATTN, seq), BF16),
            jax.ShapeDtypeStruct((n, IDX_DIM), BF16),
            jax.ShapeDtypeStruct((batch * N_IDX_HEADS, seq), F32),
        ],
        compiler_params=pltpu.CompilerParams(
            dimension_semantics=("arbitrary", "arbitrary"), vmem_limit_bytes=VMEM_LIMIT),
        name="inproj",
    )(x2, w_attn, cos_t, sin_t, cos_t.T, sin_t.T, kig, kib)


def _key_to_f32(key):
    bits = key ^ ((key >> 31) & jnp.int32(0x7FFFFFFF))
    return lax.bitcast_convert_type(bits, F32)


def _fold_rows(x, op):
    rows, cols = x.shape
    return op(x.reshape(rows // ACC_ROWS, ACC_ROWS, cols), axis=0)


def _dsa_kernel(qit_ref, wit_ref, ki_ref, qt_ref, k_ref, vt_ref, o_ref,
                isc_s, qz_s, thr_s, need_s, m_s, mblk_s, alpha_s, acc_s, s_st, p_st, *, tq, tk, topk, seq):
    j = pl.program_id(1)
    t0 = j * tq
    n_kb = (t0 + tq + tk - 1) // tk

    chunk_shift = CHUNK.bit_length() - 1
    q_chunk = (t0 + lax.broadcasted_iota(jnp.int32, (tk, tq), 1)) >> chunk_shift
    key_off = lax.broadcasted_iota(jnp.int32, (tk, tq), 0)

    def score_block(kb, carry, on_diagonal):
        hi_part, lo_part, pos_part, nonneg_part = carry
        kib = ki_ref[pl.ds(pl.multiple_of(kb * tk, tk), tk), :]
        isc = jnp.zeros((tk, tq), F32)
        for h in range(N_IDX_HEADS):
            lg = _dot(kib, qit_ref[h * IDX_DIM:(h + 1) * IDX_DIM, :])
            isc = isc + jnp.maximum(lg, 0.0) * wit_ref[h:h + 1, :]
        if on_diagonal:
            admissible = ((kb * tk + key_off) >> chunk_shift) <= q_chunk
            masked = jnp.where(admissible, isc, -jnp.inf)
        else:
            masked = isc
        isc_s[kb] = masked
        return (jnp.maximum(hi_part, _fold_rows(isc, jnp.max)),
                jnp.minimum(lo_part, _fold_rows(isc, jnp.min)),
                pos_part + _fold_rows(jnp.where(masked > 0.0, 1, 0), jnp.sum),
                nonneg_part + _fold_rows(jnp.where(masked >= 0.0, 1, 0), jnp.sum))

    carry = lax.fori_loop(
        0, n_kb - 1, functools.partial(score_block, on_diagonal=False),
        (jnp.full((ACC_ROWS, tq), -jnp.inf, F32), jnp.full((ACC_ROWS, tq), jnp.inf, F32),
         jnp.zeros((ACC_ROWS, tq), jnp.int32), jnp.zeros((ACC_ROWS, tq), jnp.int32)))
    hi_part, lo_part, pos_part, nonneg_part = score_block(n_kb - 1, carry, on_diagonal=True)
    s_max = jnp.max(hi_part, axis=0, keepdims=True)
    s_min = jnp.min(lo_part, axis=0, keepdims=True)
    n_pos = jnp.sum(pos_part, axis=0, keepdims=True)
    n_nonneg = jnp.sum(nonneg_part, axis=0, keepdims=True)

    def count_keys(pred_fn):
        def body(kb, part):
            hit = jnp.where(pred_fn(kb, isc_s[kb]), 1, 0)
            return part + _fold_rows(hit, jnp.sum)
        part = lax.fori_loop(0, n_kb, body, jnp.zeros((ACC_ROWS, tq), jnp.int32))
        return jnp.sum(part, axis=0, keepdims=True)

    q_pos = t0 + lax.broadcasted_iota(jnp.int32, (1, tq), 1)
    n_admissible = ((q_pos >> chunk_shift) + 1) << chunk_shift
    keep_all = n_admissible <= topk
    zero_thr = jnp.logical_and(jnp.logical_not(keep_all),
                               jnp.logical_and(n_pos < topk, n_nonneg >= topk))
    pinned = jnp.logical_or(keep_all, zero_thr)
    positive = n_pos >= topk
    pin = jnp.where(keep_all, s_min, 0.0)
    lo = jnp.where(pinned, pin, jnp.where(positive, 0.0, s_min))
    hi = jnp.where(pinned, pin, jnp.where(positive, s_max, 0.0))
    n_lo = jnp.where(jnp.logical_or(keep_all, jnp.logical_not(jnp.logical_or(zero_thr, positive))),
                     n_admissible, n_nonneg)

    def bisect_step(_, carry):
        lo, hi, n_lo = carry
        mid = lo + (hi - lo) * 0.5
        cnt = count_keys(lambda kb, blk: blk >= mid)
        keep = cnt >= topk
        return jnp.where(keep, mid, lo), jnp.where(keep, hi, mid), jnp.where(keep, cnt, n_lo)

    lo, _, n_lo = lax.fori_loop(0, BISECT_PASSES, bisect_step, (lo, hi, n_lo))
    thr_s[...] = jnp.broadcast_to(lo, thr_s.shape)
    unresolved = jnp.max(jnp.where(jnp.logical_or(zero_thr, n_lo <= topk), 0, 1)) > 0
    zero_ties = jnp.logical_and(zero_thr, n_nonneg > topk)
    need_s[...] = jnp.broadcast_to(jnp.where(zero_ties, topk - n_pos, seq), need_s.shape)
    tied = jnp.max(jnp.where(zero_ties, 1, 0)) > 0

    def to_bias(select_fn):
        def body(kb, carry):
            isc_s[kb] = jnp.where(select_fn(kb, isc_s[kb]), 0.0, NEG_INF)
            return carry
        lax.fori_loop(0, n_kb, body, 0)

    def to_bias_ranked(thr_v, need):
        lower_tri = jnp.where(lax.broadcasted_iota(jnp.int32, (tk, tk), 0)
                              >= lax.broadcasted_iota(jnp.int32, (tk, tk), 1), 1.0, 0.0).astype(BF16)
        need_f = need.astype(F32)

        def body(kb, before):
            blk = isc_s[kb]
            tie = blk == thr_v
            rank = before + _dot(lower_tri, jnp.where(tie, 1.0, 0.0).astype(BF16))
            keep = jnp.logical_or(blk > thr_v, jnp.logical_and(tie, rank <= need_f))
            isc_s[kb] = jnp.where(keep, 0.0, NEG_INF)
            return rank[tk - 1:tk, :]
        lax.fori_loop(0, n_kb, body, jnp.zeros((1, tq), F32))

    @pl.when(jnp.logical_not(jnp.logical_or(unresolved, tied)))
    def _():
        thr_v = thr_s[0:1, :]
        to_bias(lambda kb, blk: blk >= thr_v)

    @pl.when(jnp.logical_and(jnp.logical_not(unresolved), tied))
    def _():
        to_bias_ranked(thr_s[0:1, :], need_s[0:1, :])

    @pl.when(unresolved)
    def _():
        def bit_step(i, carry):
            thr_key, n_ge = carry
            cand = jnp.where(i == 0, jnp.zeros_like(thr_key), thr_key | (jnp.int32(1) << (31 - i)))
            cand_f = _key_to_f32(jnp.maximum(cand, KEY_NEG_FLT_MAX))
            cnt = count_keys(lambda kb, blk: blk >= cand_f)
            accept = cnt >= topk
            return jnp.where(accept, cand, thr_key), jnp.where(accept, cnt, n_ge)

        thr_key, n_ge = lax.fori_loop(
            0, 32, bit_step, (jnp.full((1, tq), INT_MIN, jnp.int32), jnp.zeros((1, tq), jnp.int32)))
        thr_v = _key_to_f32(jnp.maximum(thr_key, KEY_NEG_FLT_MAX))
        has_ties = jnp.max(n_ge) > topk

        @pl.when(jnp.logical_not(has_ties))
        def _():
            to_bias(lambda kb, blk: blk >= thr_v)

        @pl.when(has_ties)
        def _():
            n_gt = count_keys(lambda kb, blk: blk > thr_v)
            to_bias_ranked(thr_v, jnp.where(n_ge > topk, topk - n_gt, seq))

    zeros_half = jnp.zeros((HEAD_DIM, tq), BF16)
    for h in range(N_HEADS):
        head_rows = qt_ref[h * HEAD_DIM:(h + 1) * HEAD_DIM, :]
        pair = [head_rows, zeros_half] if h % 2 == 0 else [zeros_half, head_rows]
        qz_s[h] = jnp.concatenate(pair, axis=0)
    m_s[...] = jnp.full(m_s.shape, M_FLOOR, F32)
    acc_s[...] = jnp.zeros(acc_s.shape, F32)

    chunks = [slice(c, c + SOFTMAX_ROWS) for c in range(0, tk, SOFTMAX_ROWS)]

    def scores(kb, h):
        rows = pl.ds(pl.multiple_of(kb * tk, tk), tk)
        slab = slice((h // 2) * LANES, (h // 2 + 1) * LANES)
        s = _dot(k_ref[rows, slab], qz_s[h]) + isc_s[kb]
        s_st[h] = s
        mblk_s[h] = jnp.broadcast_to(
            jnp.max(_fold_rows(s, jnp.max), axis=0, keepdims=True), (SUBLANES, tq))

    def probs(h):
        m_old = m_s[h]
        m_new = jnp.maximum(m_old, mblk_s[h])
        alpha_s[h] = jnp.exp2(m_old - m_new)
        for c in chunks:
            p_st[h, c, :] = jnp.exp2(s_st[h, c, :] - m_new[0:1, :]).astype(BF16)
        m_s[h] = m_new

    def accum(kb, h):
        pv = _dot(vt_ref[kb, h * VT_ROWS:(h + 1) * VT_ROWS, :], p_st[h])
        acc_s[h] = alpha_s[h][0:1, :] * acc_s[h] + pv

    for h in range(N_HEADS):
        scores(0, h)

    @pl.when(n_kb < 2)
    def _():
        for h in range(N_HEADS):
            probs(h)

    @pl.when(n_kb >= 2)
    def _():
        for h in range(N_HEADS):
            probs(h)
        for h in range(N_HEADS):
            scores(1, h)

    def steady(t, carry):
        for h in range(N_HEADS):
            accum(t - 2, h)
        for h in range(N_HEADS):
            probs(h)
        for h in range(N_HEADS):
            scores(t, h)
        return carry

    lax.fori_loop(2, n_kb, steady, 0)

    @pl.when(n_kb >= 2)
    def _():
        for h in range(N_HEADS):
            accum(n_kb - 2, h)
        for h in range(N_HEADS):
            probs(h)

    for h in range(N_HEADS):
        accum(n_kb - 1, h)

    outs = []
    for h in range(N_HEADS):
        a = acc_s[h]
        outs.append(a[:HEAD_DIM] / a[HEAD_DIM:HEAD_DIM + 1])
    o_ref[...] = jnp.concatenate(outs, axis=0).T.astype(BF16)


def _dsa(qit, wit, ki, qt, k, vt, *, batch, seq, tq, tk):
    assert tk % tq == 0 and tq % CHUNK == 0 and seq % tk == 0, (seq, tq, tk)
    n = batch * seq
    nt = seq // tq
    topk = min(TOPK_MAX, seq // 4)
    n_blk = seq // tk
    qcol = lambda b, j: (b, j)
    whole = lambda b, j: (b, 0)
    kern = functools.partial(_dsa_kernel, tq=tq, tk=tk, topk=topk, seq=seq)
    return pl.pallas_call(
        kern,
        grid=(batch, nt),
        in_specs=[
            pl.BlockSpec((D_ATTN, tq), qcol),
            pl.BlockSpec((N_IDX_HEADS, tq), qcol),
            pl.BlockSpec((seq, IDX_DIM), whole),
            pl.BlockSpec((D_ATTN, tq), qcol),
            pl.BlockSpec((seq, D_ATTN), whole),
            pl.BlockSpec((None, n_blk, N_HEADS * VT_ROWS, tk), lambda b, j: (b, 0, 0, 0)),
        ],
        out_specs=pl.BlockSpec((tq, D_ATTN), lambda b, j: (b * nt + j, 0)),
        out_shape=jax.ShapeDtypeStruct((n, D_ATTN), BF16),
        scratch_shapes=[
            pltpu.VMEM((n_blk, tk, tq), F32),
            pltpu.VMEM((N_HEADS, LANES, tq), BF16),
            pltpu.VMEM((SUBLANES, tq), F32),
            pltpu.VMEM((SUBLANES, tq), jnp.int32),
            pltpu.VMEM((N_HEADS, SUBLANES, tq), F32),
            pltpu.VMEM((N_HEADS, SUBLANES, tq), F32),
            pltpu.VMEM((N_HEADS, SUBLANES, tq), F32),
            pltpu.VMEM((N_HEADS, VT_ROWS, tq), F32),
            pltpu.VMEM((N_HEADS, tk, tq), F32),
            pltpu.VMEM((N_HEADS, tk, tq), BF16),
        ],
        compiler_params=pltpu.CompilerParams(
            dimension_semantics=("arbitrary", "arbitrary"), vmem_limit_bytes=VMEM_LIMIT),
        name="dsa",
    )(qit, wit, ki, qt, k, vt)


def _mixer_kernel(x_ref, xh_ref, at_ref, wab_ref, dww_ref, dwb_ref, cg_ref, cb_ref,
                  wg_ref, gb_ref, wco_ref, wao_ref, wout_ref, g_ref, b_ref, o_ref,
                  u_s, ush_s, c_s, *, tm):
    j = pl.program_id(1)
    x = x_ref[...]
    xb = x.astype(BF16)

    def glu(rows_bf16):
        ab = _dot(rows_bf16, wab_ref[...])
        return ab[:, :D_CONV] * jax.nn.sigmoid(ab[:, D_CONV:])

    u_halo = glu(xh_ref[...].astype(BF16))
    u_s[0:CONV_HALO, :] = jnp.where(j > 0, u_halo, 0.0)
    u_s[CONV_HALO:CONV_HALO + tm, :] = glu(xb)

    gate_c = jax.nn.sigmoid(_dot(xb, wg_ref[:, :D_MODEL]) + gb_ref[:, :D_MODEL])
    gate_a = jax.nn.sigmoid(_dot(xb, wg_ref[:, D_MODEL:]) + gb_ref[:, D_MODEL:])
    attn_part = gate_a * _dot(at_ref[...], wao_ref[...])

    shifted_rows = CONV_HALO + tm - SUBLANES
    for r in range(1, SUBLANES):
        ush_s[r - 1, 0:shifted_rows, :] = u_s[r:r + shifted_rows, :]

    rows_per_step = 64
    first = CONV_HALO - (CONV_WIDTH - 1)
    for r0 in range(0, tm, rows_per_step):
        for c0 in range(0, D_CONV, LANES):
            acc = jnp.broadcast_to(dwb_ref[:, c0:c0 + LANES], (rows_per_step, LANES))
            for tap in range(CONV_WIDTH):
                phase = (first + tap) % SUBLANES
                base = r0 + (first + tap) - phase
                src = u_s if phase == 0 else ush_s.at[phase - 1]
                acc = acc + (src[base:base + rows_per_step, c0:c0 + LANES]
                             * dww_ref[tap:tap + 1, c0:c0 + LANES])
            c_s[r0:r0 + rows_per_step, c0:c0 + LANES] = acc

    y = _layer_norm(c_s[...], cg_ref[...], cb_ref[...])
    conv_feat = (y * jax.nn.sigmoid(y)).astype(BF16)

    merged = gate_c * _dot(conv_feat, wco_ref[...]) + attn_part
    mixer = _dot(merged.astype(BF16), wout_ref[...])
    o_ref[...] = _layer_norm(DEEPNORM_ALPHA * x + mixer, g_ref[...], b_ref[...])


def _mixer(x2, at, w_ab, dw_w, dw_b, conv_g, conv_b, w_g, gate_b, w_co, w_ao, w_out, ln_g, ln_b,
           *, batch, seq, tm):
    n = batch * seq
    nt = seq // tm
    halo_per_tile = tm // CONV_HALO
    halo_per_seq = seq // CONV_HALO
    row = lambda b, j: (b * nt + j, 0)
    halo = lambda b, j: (jnp.maximum(b * halo_per_seq + j * halo_per_tile - 1, 0), 0)
    const = lambda b, j: (0, 0)
    return pl.pallas_call(
        functools.partial(_mixer_kernel, tm=tm),
        grid=(batch, nt),
        in_specs=[
            pl.BlockSpec((tm, D_MODEL), row),
            pl.BlockSpec((CONV_HALO, D_MODEL), halo),
            pl.BlockSpec((tm, D_ATTN), row),
            pl.BlockSpec((D_MODEL, 2 * D_CONV), const),
            pl.BlockSpec((CONV_HALO, D_CONV), const),
            pl.BlockSpec((1, D_CONV), const),
            pl.BlockSpec((1, D_CONV), const),
            pl.BlockSpec((1, D_CONV), const),
            pl.BlockSpec((D_MODEL, 2 * D_MODEL), const),
            pl.BlockSpec((1, 2 * D_MODEL), const),
            pl.BlockSpec((D_CONV, D_MODEL), const),
            pl.BlockSpec((D_ATTN, D_MODEL), const),
            pl.BlockSpec((D_MODEL, D_MODEL), const),
            pl.BlockSpec((1, D_MODEL), const),
            pl.BlockSpec((1, D_MODEL), const),
        ],
        out_specs=pl.BlockSpec((tm, D_MODEL), row),
        out_shape=jax.ShapeDtypeStruct((n, D_MODEL), F32),
        scratch_shapes=[
            pltpu.VMEM((CONV_HALO + tm, D_CONV), F32),
            pltpu.VMEM((SUBLANES - 1, CONV_HALO + tm, D_CONV), F32),
            pltpu.VMEM((tm, D_CONV), F32),
        ],
        compiler_params=pltpu.CompilerParams(
            dimension_semantics=("arbitrary", "arbitrary"), vmem_limit_bytes=VMEM_LIMIT),
        name="mixer",
    )(x2, x2, at, w_ab, dw_w, dw_b, conv_g, conv_b, w_g, gate_b, w_co, w_ao, w_out, ln_g, ln_b)


def _ffn_kernel(x_ref, wi_ref, wo_ref, g_ref, b_ref, o_ref, *, ff_chunk):
    x = x_ref[...]
    xb = x.astype(BF16)
    acc = jnp.zeros(x.shape, F32)
    for c0 in range(0, D_FF, ff_chunk):
        h = jnp.maximum(_dot(xb, wi_ref[:, c0:c0 + ff_chunk]), 0.0)
        acc = acc + _dot((h * h).astype(BF16), wo_ref[c0:c0 + ff_chunk, :])
    o_ref[...] = _layer_norm(DEEPNORM_ALPHA * x + acc, g_ref[...], b_ref[...])


def _ffn(x2, w_i, w_o, ln_g, ln_b, *, tm, ff_chunk):
    n = x2.shape[0]
    row = lambda i: (i, 0)
    const = lambda i: (0, 0)
    return pl.pallas_call(
        functools.partial(_ffn_kernel, ff_chunk=ff_chunk),
        grid=(n // tm,),
        in_specs=[
            pl.BlockSpec((tm, D_MODEL), row),
            pl.BlockSpec((D_MODEL, D_FF), const, pipeline_mode=pl.Buffered(1)),
            pl.BlockSpec((D_FF, D_MODEL), const, pipeline_mode=pl.Buffered(1)),
            pl.BlockSpec((1, D_MODEL), const),
            pl.BlockSpec((1, D_MODEL), const),
        ],
        out_specs=pl.BlockSpec((tm, D_MODEL), row),
        out_shape=jax.ShapeDtypeStruct((n, D_MODEL), F32),
        compiler_params=pltpu.CompilerParams(
            dimension_semantics=("arbitrary",), vmem_limit_bytes=VMEM_LIMIT),
        name="ffn",
    )(x2, w_i, w_o, ln_g, ln_b)


def _rope_tables(seq):
    inv_freq = ROPE_THETA ** (-jnp.arange(0, HEAD_DIM, 2, dtype=F32) / HEAD_DIM)
    ang = jnp.arange(seq, dtype=jnp.int32).astype(F32)[:, None] * inv_freq[None, :]
    cos = jnp.cos(ang)
    sin = jnp.sin(ang)
    cos_blk = jnp.concatenate([cos, cos], axis=1)
    sin_blk = jnp.concatenate([-sin, sin], axis=1)
    return jnp.tile(cos_blk, (1, N_HEADS)), jnp.tile(sin_blk, (1, N_HEADS))


def _pad_lanes(v, width):
    return jnp.pad(v, (0, width - v.shape[0])).reshape(1, width)


def kernel(x, w_in, dw_w, dw_b, conv_ln_g, conv_ln_b, w_conv_out, idx_k_ln_g, idx_k_ln_b,
           w_attn_out, gate_b, w_out, ln1_g, ln1_b, w_ff_in, w_ff_out, ln2_g, ln2_b):
    batch, seq, _ = x.shape
    cos_t, sin_t = _rope_tables(seq)
    tm = min(512, seq)
    tq = min(256, seq)
    tk = min(512, seq)

    o_a, o_b, o_q, o_k, o_v = 0, D_CONV, 2 * D_CONV, 2 * D_CONV + D_ATTN, 2 * D_CONV + 2 * D_ATTN
    o_qi = o_v + D_ATTN
    o_ki = o_qi + N_IDX_HEADS * IDX_DIM
    o_wi = o_ki + IDX_DIM
    o_gc = o_wi + N_IDX_HEADS
    o_ga = o_gc + D_MODEL

    h = x.reshape(batch * seq, D_MODEL)
    for layer in range(w_in.shape[0]):
        w = w_in[layer]
        small = jnp.pad(w[:, o_ki:o_gc], ((0, 0), (0, LANES - (o_gc - o_ki))))
        w_attn = jnp.concatenate([w[:, o_q:o_ki], small], axis=1).astype(BF16)
        w_ab = w[:, o_a:o_q].astype(BF16)
        w_g = w[:, o_gc:o_ga + D_MODEL].astype(BF16)

        qt, k, vt, qit, ki, wit = _inproj(
            h, w_attn, cos_t, sin_t,
            _pad_lanes(idx_k_ln_g[layer], LANES), _pad_lanes(idx_k_ln_b[layer], LANES),
            batch=batch, seq=seq, tm=tk)
        attn = _dsa(qit, wit, ki, qt, k, vt, batch=batch, seq=seq, tq=tq, tk=tk)

        dww = jnp.pad(dw_w[layer], ((0, CONV_HALO - CONV_WIDTH), (0, 0)))
        h = _mixer(h, attn, w_ab, dww, dw_b[layer].reshape(1, D_CONV),
                   conv_ln_g[layer].reshape(1, D_CONV), conv_ln_b[layer].reshape(1, D_CONV),
                   w_g, gate_b[layer].reshape(1, 2 * D_MODEL),
                   w_conv_out[layer].astype(BF16), w_attn_out[layer].astype(BF16),
                   w_out[layer].astype(BF16),
                   ln1_g[layer].reshape(1, D_MODEL), ln1_b[layer].reshape(1, D_MODEL),
                   batch=batch, seq=seq, tm=tm)
        h = _ffn(h, w_ff_in[layer].astype(BF16), w_ff_out[layer].astype(BF16),
                 ln2_g[layer].reshape(1, D_MODEL), ln2_b[layer].reshape(1, D_MODEL),
                 tm=tm, ff_chunk=1024)
    return h.reshape(batch, seq, D_MODEL)
```

```python
import functools

import jax
import jax.numpy as jnp
from jax import lax
from jax.experimental import pallas as pl
from jax.experimental.pallas import tpu as pltpu

D_MODEL = 1024
CHUNK = 64
D_CONV = 512
CONV_WIDTH = 31
N_HEADS = 8
HEAD_DIM = 64
D_ATTN = N_HEADS * HEAD_DIM
N_IDX_HEADS = 8
IDX_DIM = 64
TOPK_MAX = 256
D_FF = 4 * D_MODEL
ROPE_THETA = 10000.0
LN_EPS = 1e-5
NEG_INF = -1e30
DEPTH = 1
DEEPNORM_ALPHA = (2.0 * DEPTH) ** 0.25

LANES = 128
SUBLANES = 8
CONV_HALO = 32
ONES_ROWS = 16
VT_ROWS = HEAD_DIM + ONES_ROWS
ACC_ROWS = 4 * SUBLANES
BISECT_PASSES = 26
SOFTMAX_ROWS = 128
LOG2_E = 1.4426950408889634
INT_MIN = -(2 ** 31)
KEY_NEG_FLT_MAX = -(2 ** 31) + 0x00800000
M_FLOOR = -1e20
VMEM_LIMIT = 56 * 1024 * 1024

F32 = jnp.float32
BF16 = jnp.bfloat16


def _dot(a, b):
    return jnp.dot(a, b, preferred_element_type=F32)


def _layer_norm(x, g, b):
    mu = jnp.mean(x, axis=-1, keepdims=True)
    d = x - mu
    var = jnp.mean(d * d, axis=-1, keepdims=True)
    return d * lax.rsqrt(var + LN_EPS) * g + b


def _rope(p, cos, sin_signed):
    width = p.shape[-1]
    lane = lax.broadcasted_iota(jnp.int32, p.shape, 1)
    first_half = (lane % HEAD_DIM) < (HEAD_DIM // 2)
    swapped = jnp.where(first_half,
                        pltpu.roll(p, width - HEAD_DIM // 2, 1),
                        pltpu.roll(p, HEAD_DIM // 2, 1))
    return p * cos + swapped * sin_signed


def _rope_transposed(pt, cos_t, sin_signed_t):
    half = HEAD_DIM // 2
    blocks = []
    for r0 in range(0, pt.shape[0], HEAD_DIM):
        blocks += [pt[r0 + half:r0 + HEAD_DIM], pt[r0:r0 + half]]
    return pt * cos_t + jnp.concatenate(blocks, axis=0) * sin_signed_t


def _inproj_kernel(x_ref, w_ref, cos_ref, sin_ref, cost_ref, sint_ref, kig_ref, kib_ref,
                   qt_ref, k_ref, vt_ref, qit_ref, ki_ref, wit_ref):
    xb = x_ref[...].astype(BF16)
    tm = xb.shape[0]
    cos = cos_ref[...]
    sin = sin_ref[...]
    cos_t = cost_ref[...]
    sin_t = sint_ref[...]
    qt = _rope_transposed(_dot(xb, w_ref[:, 0:D_ATTN]).T, cos_t, sin_t)
    qt_ref[...] = (qt * (HEAD_DIM ** -0.5 * LOG2_E)).astype(BF16)
    k_ref[...] = _rope(_dot(xb, w_ref[:, D_ATTN:2 * D_ATTN]), cos, sin).astype(BF16)

    vt = _dot(xb, w_ref[:, 2 * D_ATTN:3 * D_ATTN]).T
    ones = jnp.ones((ONES_ROWS, tm), F32)
    pieces = []
    for h in range(N_HEADS):
        pieces += [vt[h * HEAD_DIM:(h + 1) * HEAD_DIM], ones]
    vt_ref[...] = jnp.concatenate(pieces, axis=0).astype(BF16)

    qit = _rope_transposed(_dot(xb, w_ref[:, 3 * D_ATTN:4 * D_ATTN]).T, cos_t, sin_t)
    qit_ref[...] = (qit * (IDX_DIM ** -0.5)).astype(BF16)

    sm = _dot(xb, w_ref[:, 4 * D_ATTN:4 * D_ATTN + LANES])
    lane = lax.broadcasted_iota(jnp.int32, sm.shape, 1)
    is_ki = lane < IDX_DIM
    mu = jnp.sum(jnp.where(is_ki, sm, 0.0), axis=-1, keepdims=True) * (1.0 / IDX_DIM)
    d = jnp.where(is_ki, sm - mu, 0.0)
    var = jnp.sum(d * d, axis=-1, keepdims=True) * (1.0 / IDX_DIM)
    kn = d * lax.rsqrt(var + LN_EPS) * kig_ref[...] + kib_ref[...]
    kr = _rope(kn, cos[:, :LANES], sin[:, :LANES])
    ki_ref[...] = kr[:, :IDX_DIM].astype(BF16)
    wit_ref[...] = sm.T[IDX_DIM:IDX_DIM + N_IDX_HEADS] * (N_IDX_HEADS ** -0.5)


def _inproj(x2, w_attn, rope_tables, kig, kib, *, batch, seq, tm):
    n = batch * seq
    nt = seq // tm
    row = lambda j, b: (b * nt + j, 0)
    col = lambda j, b: (b, j)
    pos = lambda j, b: (j, 0)
    const = lambda j, b: (0, 0)
    wcols = w_attn.shape[1]
    return pl.pallas_call(
        _inproj_kernel,
        grid=(nt, batch),
        in_specs=[
            pl.BlockSpec((tm, D_MODEL), row),
            pl.BlockSpec((D_MODEL, wcols), const),
            pl.BlockSpec((tm, D_ATTN), pos),
            pl.BlockSpec((tm, D_ATTN), pos),
            pl.BlockSpec((D_ATTN, tm), lambda j, b: (0, j)),
            pl.BlockSpec((D_ATTN, tm), lambda j, b: (0, j)),
            pl.BlockSpec((1, LANES), const),
            pl.BlockSpec((1, LANES), const),
        ],
        out_specs=[
            pl.BlockSpec((D_ATTN, tm), col),
            pl.BlockSpec((tm, D_ATTN), row),
            pl.BlockSpec((None, None, N_HEADS * VT_ROWS, tm), lambda j, b: (b, j, 0, 0)),
            pl.BlockSpec((D_ATTN, tm), col),
            pl.BlockSpec((tm, IDX_DIM), row),
            pl.BlockSpec((N_IDX_HEADS, tm), col),
        ],
        out_shape=[
            jax.ShapeDtypeStruct((batch * D_ATTN, seq), BF16),
            jax.ShapeDtypeStruct((n, D_ATTN), BF16),
            jax.ShapeDtypeStruct((batch, nt, N_HEADS * VT_ROWS, tm), BF16),
            jax.ShapeDtypeStruct((batch * D_ATTN, seq), BF16),
            jax.ShapeDtypeStruct((n, IDX_DIM), BF16),
            jax.ShapeDtypeStruct((batch * N_IDX_HEADS, seq), F32),
        ],
        compiler_params=pltpu.CompilerParams(
            dimension_semantics=("arbitrary", "arbitrary"), vmem_limit_bytes=VMEM_LIMIT),
        name="inproj",
    )(x2, w_attn, *rope_tables, kig, kib)


def _key_to_f32(key):
    bits = key ^ ((key >> 31) & jnp.int32(0x7FFFFFFF))
    return lax.bitcast_convert_type(bits, F32)


def _fold_rows(x, op):
    rows, cols = x.shape
    return op(x.reshape(rows // ACC_ROWS, ACC_ROWS, cols), axis=0)


def _dsa_kernel(qit_ref, wit_ref, ki_ref, qt_ref, k_ref, vt_ref, o_ref,
                isc_s, qz_s, thr_s, need_s, m_s, mblk_s, alpha_s, acc_s, s_st, p_st, *, tq, tk, topk, seq):
    j = pl.program_id(1)
    t0 = j * tq
    n_kb = (t0 + tq + tk - 1) // tk

    chunk_shift = CHUNK.bit_length() - 1
    q_chunk = (t0 + lax.broadcasted_iota(jnp.int32, (tk, tq), 1)) >> chunk_shift
    key_off = lax.broadcasted_iota(jnp.int32, (tk, tq), 0)

    def score_block(kb, carry, on_diagonal):
        hi_part, lo_part, pos_part, nonneg_part = carry
        kib = ki_ref[pl.ds(pl.multiple_of(kb * tk, tk), tk), :]
        isc = jnp.zeros((tk, tq), F32)
        for h in range(N_IDX_HEADS):
            lg = _dot(kib, qit_ref[h * IDX_DIM:(h + 1) * IDX_DIM, :])
            isc = isc + jnp.maximum(lg, 0.0) * wit_ref[h:h + 1, :]
        if on_diagonal:
            admissible = ((kb * tk + key_off) >> chunk_shift) <= q_chunk
            masked = jnp.where(admissible, isc, -jnp.inf)
        else:
            masked = isc
        isc_s[kb] = masked
        return (jnp.maximum(hi_part, _fold_rows(isc, jnp.max)),
                jnp.minimum(lo_part, _fold_rows(isc, jnp.min)),
                pos_part + _fold_rows(jnp.where(masked > 0.0, 1, 0), jnp.sum),
                nonneg_part + _fold_rows(jnp.where(masked >= 0.0, 1, 0), jnp.sum))

    carry = lax.fori_loop(
        0, n_kb - 1, functools.partial(score_block, on_diagonal=False),
        (jnp.full((ACC_ROWS, tq), -jnp.inf, F32), jnp.full((ACC_ROWS, tq), jnp.inf, F32),
         jnp.zeros((ACC_ROWS, tq), jnp.int32), jnp.zeros((ACC_ROWS, tq), jnp.int32)))
    hi_part, lo_part, pos_part, nonneg_part = score_block(n_kb - 1, carry, on_diagonal=True)
    s_max = jnp.max(hi_part, axis=0, keepdims=True)
    s_min = jnp.min(lo_part, axis=0, keepdims=True)
    n_pos = jnp.sum(pos_part, axis=0, keepdims=True)
    n_nonneg = jnp.sum(nonneg_part, axis=0, keepdims=True)

    def count_keys(pred_fn):
        def body(kb, part):
            hit = jnp.where(pred_fn(kb, isc_s[kb]), 1, 0)
            return part + _fold_rows(hit, jnp.sum)
        part = lax.fori_loop(0, n_kb, body, jnp.zeros((ACC_ROWS, tq), jnp.int32))
        return jnp.sum(part, axis=0, keepdims=True)

    q_pos = t0 + lax.broadcasted_iota(jnp.int32, (1, tq), 1)
    n_admissible = ((q_pos >> chunk_shift) + 1) << chunk_shift
    keep_all = n_admissible <= topk
    zero_thr = jnp.logical_and(jnp.logical_not(keep_all),
                               jnp.logical_and(n_pos < topk, n_nonneg >= topk))
    pinned = jnp.logical_or(keep_all, zero_thr)
    positive = n_pos >= topk
    pin = jnp.where(keep_all, s_min, 0.0)
    lo = jnp.where(pinned, pin, jnp.where(positive, 0.0, s_min))
    hi = jnp.where(pinned, pin, jnp.where(positive, s_max, 0.0))
    n_lo = jnp.where(jnp.logical_or(keep_all, jnp.logical_not(jnp.logical_or(zero_thr, positive))),
                     n_admissible, n_nonneg)

    def bisect_step(_, carry):
        lo, hi, n_lo = carry
        mid = lo + (hi - lo) * 0.5
        cnt = count_keys(lambda kb, blk: blk >= mid)
        keep = cnt >= topk
        return jnp.where(keep, mid, lo), jnp.where(keep, hi, mid), jnp.where(keep, cnt, n_lo)

    lo, _, n_lo = lax.fori_loop(0, BISECT_PASSES, bisect_step, (lo, hi, n_lo))
    thr_s[...] = jnp.broadcast_to(lo, thr_s.shape)
    unresolved = jnp.max(jnp.where(jnp.logical_or(zero_thr, n_lo <= topk), 0, 1)) > 0
    zero_ties = jnp.logical_and(zero_thr, n_nonneg > topk)
    need_s[...] = jnp.broadcast_to(jnp.where(zero_ties, topk - n_pos, seq), need_s.shape)
    tied = jnp.max(jnp.where(zero_ties, 1, 0)) > 0

    def to_bias(select_fn):
        def body(kb, carry):
            isc_s[kb] = jnp.where(select_fn(kb, isc_s[kb]), 0.0, NEG_INF)
            return carry
        lax.fori_loop(0, n_kb, body, 0)

    def to_bias_ranked(thr_v, need):
        lower_tri = jnp.where(lax.broadcasted_iota(jnp.int32, (tk, tk), 0)
                              >= lax.broadcasted_iota(jnp.int32, (tk, tk), 1), 1.0, 0.0).astype(BF16)
        need_f = need.astype(F32)

        def body(kb, before):
            blk = isc_s[kb]
            tie = blk == thr_v
            rank = before + _dot(lower_tri, jnp.where(tie, 1.0, 0.0).astype(BF16))
            keep = jnp.logical_or(blk > thr_v, jnp.logical_and(tie, rank <= need_f))
            isc_s[kb] = jnp.where(keep, 0.0, NEG_INF)
            return rank[tk - 1:tk, :]
        lax.fori_loop(0, n_kb, body, jnp.zeros((1, tq), F32))

    @pl.when(jnp.logical_not(jnp.logical_or(unresolved, tied)))
    def _():
        thr_v = thr_s[0:1, :]
        to_bias(lambda kb, blk: blk >= thr_v)

    @pl.when(jnp.logical_and(jnp.logical_not(unresolved), tied))
    def _():
        to_bias_ranked(thr_s[0:1, :], need_s[0:1, :])

    @pl.when(unresolved)
    def _():
        def bit_step(i, carry):
            thr_key, n_ge = carry
            cand = jnp.where(i == 0, jnp.zeros_like(thr_key), thr_key | (jnp.int32(1) << (31 - i)))
            cand_f = _key_to_f32(jnp.maximum(cand, KEY_NEG_FLT_MAX))
            cnt = count_keys(lambda kb, blk: blk >= cand_f)
            accept = cnt >= topk
            return jnp.where(accept, cand, thr_key), jnp.where(accept, cnt, n_ge)

        thr_key, n_ge = lax.fori_loop(
            0, 32, bit_step, (jnp.full((1, tq), INT_MIN, jnp.int32), jnp.zeros((1, tq), jnp.int32)))
        thr_v = _key_to_f32(jnp.maximum(thr_key, KEY_NEG_FLT_MAX))
        has_ties = jnp.max(n_ge) > topk

        @pl.when(jnp.logical_not(has_ties))
        def _():
            to_bias(lambda kb, blk: blk >= thr_v)

        @pl.when(has_ties)
        def _():
            n_gt = count_keys(lambda kb, blk: blk > thr_v)
            to_bias_ranked(thr_v, jnp.where(n_ge > topk, topk - n_gt, seq))

    zeros_half = jnp.zeros((HEAD_DIM, tq), BF16)
    for h in range(N_HEADS):
        head_rows = qt_ref[h * HEAD_DIM:(h + 1) * HEAD_DIM, :]
        pair = [head_rows, zeros_half] if h % 2 == 0 else [zeros_half, head_rows]
        qz_s[h] = jnp.concatenate(pair, axis=0)
    m_s[...] = jnp.full(m_s.shape, M_FLOOR, F32)
    acc_s[...] = jnp.zeros(acc_s.shape, F32)

    chunks = [slice(c, c + SOFTMAX_ROWS) for c in range(0, tk, SOFTMAX_ROWS)]

    def scores(kb, h):
        rows = pl.ds(pl.multiple_of(kb * tk, tk), tk)
        slab = slice((h // 2) * LANES, (h // 2 + 1) * LANES)
        s = _dot(k_ref[rows, slab], qz_s[h]) + isc_s[kb]
        s_st[h] = s
        mblk_s[h] = jnp.broadcast_to(
            jnp.max(_fold_rows(s, jnp.max), axis=0, keepdims=True), (SUBLANES, tq))

    def probs(h):
        m_old = m_s[h]
        m_new = jnp.maximum(m_old, mblk_s[h])
        alpha_s[h] = jnp.exp2(m_old - m_new)
        for c in chunks:
            p_st[h, c, :] = jnp.exp2(s_st[h, c, :] - m_new[0:1, :]).astype(BF16)
        m_s[h] = m_new

    def accum(kb, h):
        pv = _dot(vt_ref[kb, h * VT_ROWS:(h + 1) * VT_ROWS, :], p_st[h])
        acc_s[h] = alpha_s[h][0:1, :] * acc_s[h] + pv

    for h in range(N_HEADS):
        scores(0, h)

    @pl.when(n_kb < 2)
    def _():
        for h in range(N_HEADS):
            probs(h)

    @pl.when(n_kb >= 2)
    def _():
        for h in range(N_HEADS):
            probs(h)
        for h in range(N_HEADS):
            scores(1, h)

    def steady(t, carry):
        for h in range(N_HEADS):
            accum(t - 2, h)
        for h in range(N_HEADS):
            probs(h)
        for h in range(N_HEADS):
            scores(t, h)
        return carry

    lax.fori_loop(2, n_kb, steady, 0)

    @pl.when(n_kb >= 2)
    def _():
        for h in range(N_HEADS):
            accum(n_kb - 2, h)
        for h in range(N_HEADS):
            probs(h)

    for h in range(N_HEADS):
        accum(n_kb - 1, h)

    outs = []
    for h in range(N_HEADS):
        a = acc_s[h]
        outs.append(a[:HEAD_DIM] / a[HEAD_DIM:HEAD_DIM + 1])
    o_ref[...] = jnp.concatenate(outs, axis=0).T.astype(BF16)


def _dsa(qit, wit, ki, qt, k, vt, *, batch, seq, tq, tk):
    assert tk % tq == 0 and tq % CHUNK == 0 and seq % tk == 0, (seq, tq, tk)
    n = batch * seq
    nt = seq // tq
    topk = min(TOPK_MAX, seq // 4)
    n_blk = seq // tk
    qcol = lambda b, j: (b, j)
    whole = lambda b, j: (b, 0)
    kern = functools.partial(_dsa_kernel, tq=tq, tk=tk, topk=topk, seq=seq)
    return pl.pallas_call(
        kern,
        grid=(batch, nt),
        in_specs=[
            pl.BlockSpec((D_ATTN, tq), qcol),
            pl.BlockSpec((N_IDX_HEADS, tq), qcol),
            pl.BlockSpec((seq, IDX_DIM), whole),
            pl.BlockSpec((D_ATTN, tq), qcol),
            pl.BlockSpec((seq, D_ATTN), whole),
            pl.BlockSpec((None, n_blk, N_HEADS * VT_ROWS, tk), lambda b, j: (b, 0, 0, 0)),
        ],
        out_specs=pl.BlockSpec((tq, D_ATTN), lambda b, j: (b * nt + j, 0)),
        out_shape=jax.ShapeDtypeStruct((n, D_ATTN), BF16),
        scratch_shapes=[
            pltpu.VMEM((n_blk, tk, tq), F32),
            pltpu.VMEM((N_HEADS, LANES, tq), BF16),
            pltpu.VMEM((SUBLANES, tq), F32),
            pltpu.VMEM((SUBLANES, tq), jnp.int32),
            pltpu.VMEM((N_HEADS, SUBLANES, tq), F32),
            pltpu.VMEM((N_HEADS, SUBLANES, tq), F32),
            pltpu.VMEM((N_HEADS, SUBLANES, tq), F32),
            pltpu.VMEM((N_HEADS, VT_ROWS, tq), F32),
            pltpu.VMEM((N_HEADS, tk, tq), F32),
            pltpu.VMEM((N_HEADS, tk, tq), BF16),
        ],
        compiler_params=pltpu.CompilerParams(
            dimension_semantics=("arbitrary", "arbitrary"), vmem_limit_bytes=VMEM_LIMIT),
        name="dsa",
    )(qit, wit, ki, qt, k, vt)


def _mixer_kernel(x_ref, xh_ref, at_ref, wab_ref, dww_ref, dwb_ref, cg_ref, cb_ref,
                  wg_ref, gb_ref, wco_ref, wao_ref, wout_ref, g_ref, b_ref, o_ref,
                  u_s, ush_s, c_s, *, tm):
    j = pl.program_id(1)
    x = x_ref[...]
    xb = x.astype(BF16)

    def glu(rows_bf16):
        ab = _dot(rows_bf16, wab_ref[...])
        return ab[:, :D_CONV] * jax.nn.sigmoid(ab[:, D_CONV:])

    u_halo = glu(xh_ref[...].astype(BF16))
    u_s[0:CONV_HALO, :] = jnp.where(j > 0, u_halo, 0.0)
    u_s[CONV_HALO:CONV_HALO + tm, :] = glu(xb)

    gate_c = jax.nn.sigmoid(_dot(xb, wg_ref[:, :D_MODEL]) + gb_ref[:, :D_MODEL])
    gate_a = jax.nn.sigmoid(_dot(xb, wg_ref[:, D_MODEL:]) + gb_ref[:, D_MODEL:])
    attn_part = gate_a * _dot(at_ref[...], wao_ref[...])

    shifted_rows = CONV_HALO + tm - SUBLANES
    for r in range(1, SUBLANES):
        ush_s[r - 1, 0:shifted_rows, :] = u_s[r:r + shifted_rows, :]

    rows_per_step = 64
    first = CONV_HALO - (CONV_WIDTH - 1)
    for r0 in range(0, tm, rows_per_step):
        for c0 in range(0, D_CONV, LANES):
            acc = jnp.broadcast_to(dwb_ref[:, c0:c0 + LANES], (rows_per_step, LANES))
            for tap in range(CONV_WIDTH):
                phase = (first + tap) % SUBLANES
                base = r0 + (first + tap) - phase
                src = u_s if phase == 0 else ush_s.at[phase - 1]
                acc = acc + (src[base:base + rows_per_step, c0:c0 + LANES]
                             * dww_ref[tap:tap + 1, c0:c0 + LANES])
            c_s[r0:r0 + rows_per_step, c0:c0 + LANES] = acc

    y = _layer_norm(c_s[...], cg_ref[...], cb_ref[...])
    conv_feat = (y * jax.nn.sigmoid(y)).astype(BF16)

    merged = gate_c * _dot(conv_feat, wco_ref[...]) + attn_part
    mixer = _dot(merged.astype(BF16), wout_ref[...])
    o_ref[...] = _layer_norm(DEEPNORM_ALPHA * x + mixer, g_ref[...], b_ref[...])


def _mixer(x2, at, w_ab, dw_w, dw_b, conv_g, conv_b, w_g, gate_b, w_co, w_ao, w_out, ln_g, ln_b,
           *, batch, seq, tm):
    n = batch * seq
    nt = seq // tm
    halo_per_tile = tm // CONV_HALO
    halo_per_seq = seq // CONV_HALO
    row = lambda b, j: (b * nt + j, 0)
    halo = lambda b, j: (jnp.maximum(b * halo_per_seq + j * halo_per_tile - 1, 0), 0)
    const = lambda b, j: (0, 0)
    return pl.pallas_call(
        functools.partial(_mixer_kernel, tm=tm),
        grid=(batch, nt),
        in_specs=[
            pl.BlockSpec((tm, D_MODEL), row),
            pl.BlockSpec((CONV_HALO, D_MODEL), halo),
            pl.BlockSpec((tm, D_ATTN), row),
            pl.BlockSpec((D_MODEL, 2 * D_CONV), const),
            pl.BlockSpec((CONV_HALO, D_CONV), const),
            pl.BlockSpec((1, D_CONV), const),
            pl.BlockSpec((1, D_CONV), const),
            pl.BlockSpec((1, D_CONV), const),
            pl.BlockSpec((D_MODEL, 2 * D_MODEL), const),
            pl.BlockSpec((1, 2 * D_MODEL), const),
            pl.BlockSpec((D_CONV, D_MODEL), const),
            pl.BlockSpec((D_ATTN, D_MODEL), const),
            pl.BlockSpec((D_MODEL, D_MODEL), const),
            pl.BlockSpec((1, D_MODEL), const),
            pl.BlockSpec((1, D_MODEL), const),
        ],
        out_specs=pl.BlockSpec((tm, D_MODEL), row),
        out_shape=jax.ShapeDtypeStruct((n, D_MODEL), F32),
        scratch_shapes=[
            pltpu.VMEM((CONV_HALO + tm, D_CONV), F32),
            pltpu.VMEM((SUBLANES - 1, CONV_HALO + tm, D_CONV), F32),
            pltpu.VMEM((tm, D_CONV), F32),
        ],
        compiler_params=pltpu.CompilerParams(
            dimension_semantics=("arbitrary", "arbitrary"), vmem_limit_bytes=VMEM_LIMIT),
        name="mixer",
    )(x2, x2, at, w_ab, dw_w, dw_b, conv_g, conv_b, w_g, gate_b, w_co, w_ao, w_out, ln_g, ln_b)


def _ffn_kernel(x_ref, wi_ref, wo_ref, g_ref, b_ref, o_ref, *, ff_chunk):
    x = x_ref[...]
    xb = x.astype(BF16)
    acc = jnp.zeros(x.shape, F32)
    for c0 in range(0, D_FF, ff_chunk):
        h = jnp.maximum(_dot(xb, wi_ref[:, c0:c0 + ff_chunk]), 0.0)
        acc = acc + _dot((h * h).astype(BF16), wo_ref[c0:c0 + ff_chunk, :])
    o_ref[...] = _layer_norm(DEEPNORM_ALPHA * x + acc, g_ref[...], b_ref[...])


def _ffn(x2, w_i, w_o, ln_g, ln_b, *, tm, ff_chunk):
    n = x2.shape[0]
    row = lambda i: (i, 0)
    const = lambda i: (0, 0)
    return pl.pallas_call(
        functools.partial(_ffn_kernel, ff_chunk=ff_chunk),
        grid=(n // tm,),
        in_specs=[
            pl.BlockSpec((tm, D_MODEL), row),
            pl.BlockSpec((D_MODEL, D_FF), const, pipeline_mode=pl.Buffered(1)),
            pl.BlockSpec((D_FF, D_MODEL), const, pipeline_mode=pl.Buffered(1)),
            pl.BlockSpec((1, D_MODEL), const),
            pl.BlockSpec((1, D_MODEL), const),
        ],
        out_specs=pl.BlockSpec((tm, D_MODEL), row),
        out_shape=jax.ShapeDtypeStruct((n, D_MODEL), F32),
        compiler_params=pltpu.CompilerParams(
            dimension_semantics=("arbitrary",), vmem_limit_bytes=VMEM_LIMIT),
        name="ffn",
    )(x2, w_i, w_o, ln_g, ln_b)


def _rope_tables(seq):
    inv_freq = ROPE_THETA ** (-jnp.arange(0, HEAD_DIM, 2, dtype=F32) / HEAD_DIM)
    pos = jnp.arange(seq, dtype=jnp.int32).astype(F32)

    def tables(ang, axis):
        cos = jnp.cos(ang)
        sin = jnp.sin(ang)
        reps = (1, N_HEADS) if axis == 1 else (N_HEADS, 1)
        return (jnp.tile(jnp.concatenate([cos, cos], axis=axis), reps),
                jnp.tile(jnp.concatenate([-sin, sin], axis=axis), reps))

    return (tables(pos[:, None] * inv_freq[None, :], 1)
            + tables(inv_freq[:, None] * pos[None, :], 0))


def _tile_sizes(seq):
    return min(512, seq), min(256, seq), min(512, seq)


def _pad_lanes(v, width):
    return jnp.pad(v, (0, width - v.shape[0])).reshape(1, width)


def kernel(x, w_in, dw_w, dw_b, conv_ln_g, conv_ln_b, w_conv_out, idx_k_ln_g, idx_k_ln_b,
           w_attn_out, gate_b, w_out, ln1_g, ln1_b, w_ff_in, w_ff_out, ln2_g, ln2_b):
    batch, seq, _ = x.shape
    rope_tables = _rope_tables(seq)
    tm, tq, tk = _tile_sizes(seq)

    o_a, o_b, o_q, o_k, o_v = 0, D_CONV, 2 * D_CONV, 2 * D_CONV + D_ATTN, 2 * D_CONV + 2 * D_ATTN
    o_qi = o_v + D_ATTN
    o_ki = o_qi + N_IDX_HEADS * IDX_DIM
    o_wi = o_ki + IDX_DIM
    o_gc = o_wi + N_IDX_HEADS
    o_ga = o_gc + D_MODEL

    h = x.reshape(batch * seq, D_MODEL)
    for layer in range(w_in.shape[0]):
        w = w_in[layer]
        small = jnp.pad(w[:, o_ki:o_gc], ((0, 0), (0, LANES - (o_gc - o_ki))))
        w_attn = jnp.concatenate([w[:, o_q:o_ki], small], axis=1).astype(BF16)
        w_ab = w[:, o_a:o_q].astype(BF16)
        w_g = w[:, o_gc:o_ga + D_MODEL].astype(BF16)

        qt, k, vt, qit, ki, wit = _inproj(
            h, w_attn, rope_tables,
            _pad_lanes(idx_k_ln_g[layer], LANES), _pad_lanes(idx_k_ln_b[layer], LANES),
            batch=batch, seq=seq, tm=tk)
        attn = _dsa(qit, wit, ki, qt, k, vt, batch=batch, seq=seq, tq=tq, tk=tk)

        dww = jnp.pad(dw_w[layer], ((0, CONV_HALO - CONV_WIDTH), (0, 0)))
        h = _mixer(h, attn, w_ab, dww, dw_b[layer].reshape(1, D_CONV),
                   conv_ln_g[layer].reshape(1, D_CONV), conv_ln_b[layer].reshape(1, D_CONV),
                   w_g, gate_b[layer].reshape(1, 2 * D_MODEL),
                   w_conv_out[layer].astype(BF16), w_attn_out[layer].astype(BF16),
                   w_out[layer].astype(BF16),
                   ln1_g[layer].reshape(1, D_MODEL), ln1_b[layer].reshape(1, D_MODEL),
                   batch=batch, seq=seq, tm=tm)
        h = _ffn(h, w_ff_in[layer].astype(BF16), w_ff_out[layer].astype(BF16),
                 ln2_g[layer].reshape(1, D_MODEL), ln2_b[layer].reshape(1, D_MODEL),
                 tm=tm, ff_chunk=1024)
    return h.reshape(batch, seq, D_MODEL)
```

```python
import functools

import jax
import jax.numpy as jnp
from jax import lax
from jax.experimental import pallas as pl
from jax.experimental.pallas import tpu as pltpu

D_MODEL = 1024
CHUNK = 64
D_CONV = 512
CONV_WIDTH = 31
N_HEADS = 8
HEAD_DIM = 64
D_ATTN = N_HEADS * HEAD_DIM
N_IDX_HEADS = 8
IDX_DIM = 64
TOPK_MAX = 256
D_FF = 4 * D_MODEL
ROPE_THETA = 10000.0
LN_EPS = 1e-5
NEG_INF = -1e30
DEPTH = 1
DEEPNORM_ALPHA = (2.0 * DEPTH) ** 0.25

LANES = 128
SUBLANES = 8
CONV_HALO = 32
ONES_ROWS = 16
VT_ROWS = HEAD_DIM + ONES_ROWS
ACC_ROWS = 4 * SUBLANES
BISECT_PASSES = 25
SOFTMAX_ROWS = 128
LOG2_E = 1.4426950408889634
INT_MIN = -(2 ** 31)
KEY_NEG_FLT_MAX = -(2 ** 31) + 0x00800000
M_FLOOR = -1e20
VMEM_LIMIT = 56 * 1024 * 1024

F32 = jnp.float32
BF16 = jnp.bfloat16


def _dot(a, b):
    return jnp.dot(a, b, preferred_element_type=F32)


def _layer_norm(x, g, b):
    mu = jnp.mean(x, axis=-1, keepdims=True)
    d = x - mu
    var = jnp.mean(d * d, axis=-1, keepdims=True)
    return d * lax.rsqrt(var + LN_EPS) * g + b


def _rope(p, cos, sin_signed):
    width = p.shape[-1]
    lane = lax.broadcasted_iota(jnp.int32, p.shape, 1)
    first_half = (lane % HEAD_DIM) < (HEAD_DIM // 2)
    swapped = jnp.where(first_half,
                        pltpu.roll(p, width - HEAD_DIM // 2, 1),
                        pltpu.roll(p, HEAD_DIM // 2, 1))
    return p * cos + swapped * sin_signed


def _rope_transposed(pt, cos_t, sin_signed_t):
    half = HEAD_DIM // 2
    blocks = []
    for r0 in range(0, pt.shape[0], HEAD_DIM):
        blocks += [pt[r0 + half:r0 + HEAD_DIM], pt[r0:r0 + half]]
    return pt * cos_t + jnp.concatenate(blocks, axis=0) * sin_signed_t


def _inproj_kernel(x_ref, w_ref, cos_ref, sin_ref, cost_ref, sint_ref, kig_ref, kib_ref,
                   qt_ref, k_ref, vt_ref, qit_ref, ki_ref, wit_ref):
    xb = x_ref[...].astype(BF16)
    tm = xb.shape[0]
    cos = cos_ref[...]
    sin = sin_ref[...]
    cos_t = cost_ref[...]
    sin_t = sint_ref[...]
    qt = _rope_transposed(_dot(xb, w_ref[:, 0:D_ATTN]).T, cos_t, sin_t)
    qt_ref[...] = (qt * (HEAD_DIM ** -0.5 * LOG2_E)).astype(BF16)
    k_ref[...] = _rope(_dot(xb, w_ref[:, D_ATTN:2 * D_ATTN]), cos, sin).astype(BF16)

    vt = _dot(xb, w_ref[:, 2 * D_ATTN:3 * D_ATTN]).T
    ones = jnp.ones((ONES_ROWS, tm), F32)
    pieces = []
    for h in range(N_HEADS):
        pieces += [vt[h * HEAD_DIM:(h + 1) * HEAD_DIM], ones]
    vt_ref[...] = jnp.concatenate(pieces, axis=0).astype(BF16)

    qit = _rope_transposed(_dot(xb, w_ref[:, 3 * D_ATTN:4 * D_ATTN]).T, cos_t, sin_t)
    qit_ref[...] = (qit * (IDX_DIM ** -0.5)).astype(BF16)

    sm = _dot(xb, w_ref[:, 4 * D_ATTN:4 * D_ATTN + LANES])
    lane = lax.broadcasted_iota(jnp.int32, sm.shape, 1)
    is_ki = lane < IDX_DIM
    mu = jnp.sum(jnp.where(is_ki, sm, 0.0), axis=-1, keepdims=True) * (1.0 / IDX_DIM)
    d = jnp.where(is_ki, sm - mu, 0.0)
    var = jnp.sum(d * d, axis=-1, keepdims=True) * (1.0 / IDX_DIM)
    kn = d * lax.rsqrt(var + LN_EPS) * kig_ref[...] + kib_ref[...]
    kr = _rope(kn, cos[:, :LANES], sin[:, :LANES])
    ki_ref[...] = kr[:, :IDX_DIM].astype(BF16)
    wit_ref[...] = sm.T[IDX_DIM:IDX_DIM + N_IDX_HEADS] * (N_IDX_HEADS ** -0.5)


def _inproj(x2, w_attn, cos_t, sin_t, kig, kib, *, batch, seq, tm):
    n = batch * seq
    nt = seq // tm
    row = lambda j, b: (b * nt + j, 0)
    col = lambda j, b: (b, j)
    pos = lambda j, b: (j, 0)
    const = lambda j, b: (0, 0)
    wcols = w_attn.shape[1]
    return pl.pallas_call(
        _inproj_kernel,
        grid=(nt, batch),
        in_specs=[
            pl.BlockSpec((tm, D_MODEL), row),
            pl.BlockSpec((D_MODEL, wcols), const),
            pl.BlockSpec((tm, D_ATTN), pos),
            pl.BlockSpec((tm, D_ATTN), pos),
            pl.BlockSpec((D_ATTN, tm), lambda j, b: (0, j)),
            pl.BlockSpec((D_ATTN, tm), lambda j, b: (0, j)),
            pl.BlockSpec((1, LANES), const),
            pl.BlockSpec((1, LANES), const),
        ],
        out_specs=[
            pl.BlockSpec((D_ATTN, tm), col),
            pl.BlockSpec((tm, D_ATTN), row),
            pl.BlockSpec((None, None, N_HEADS * VT_ROWS, tm), lambda j, b: (b, j, 0, 0)),
            pl.BlockSpec((D_ATTN, tm), col),
            pl.BlockSpec((tm, IDX_DIM), row),
            pl.BlockSpec((N_IDX_HEADS, tm), col),
        ],
        out_shape=[
            jax.ShapeDtypeStruct((batch * D_ATTN, seq), BF16),
            jax.ShapeDtypeStruct((n, D_ATTN), BF16),
            jax.ShapeDtypeStruct((batch, nt, N_HEADS * VT_ROWS, tm), BF16),
            jax.ShapeDtypeStruct((batch * D_ATTN, seq), BF16),
            jax.ShapeDtypeStruct((n, IDX_DIM), BF16),
            jax.ShapeDtypeStruct((batch * N_IDX_HEADS, seq), F32),
        ],
        compiler_params=pltpu.CompilerParams(
            dimension_semantics=("arbitrary", "arbitrary"), vmem_limit_bytes=VMEM_LIMIT),
        name="inproj",
    )(x2, w_attn, cos_t, sin_t, cos_t.T, sin_t.T, kig, kib)


def _key_to_f32(key):
    bits = key ^ ((key >> 31) & jnp.int32(0x7FFFFFFF))
    return lax.bitcast_convert_type(bits, F32)


def _fold_rows(x, op):
    rows, cols = x.shape
    return op(x.reshape(rows // ACC_ROWS, ACC_ROWS, cols), axis=0)


def _dsa_kernel(qit_ref, wit_ref, ki_ref, qt_ref, k_ref, vt_ref, o_ref,
                isc_s, qz_s, thr_s, need_s, m_s, mblk_s, alpha_s, acc_s, s_st, p_st, *, tq, tk, topk, seq):
    j = pl.program_id(1)
    t0 = j * tq
    n_kb = (t0 + tq + tk - 1) // tk

    chunk_shift = CHUNK.bit_length() - 1
    q_chunk = (t0 + lax.broadcasted_iota(jnp.int32, (tk, tq), 1)) >> chunk_shift
    key_off = lax.broadcasted_iota(jnp.int32, (tk, tq), 0)

    def score_block(kb, carry, on_diagonal):
        hi_part, lo_part, pos_part, nonneg_part = carry
        kib = ki_ref[pl.ds(pl.multiple_of(kb * tk, tk), tk), :]
        isc = jnp.zeros((tk, tq), F32)
        for h in range(N_IDX_HEADS):
            lg = _dot(kib, qit_ref[h * IDX_DIM:(h + 1) * IDX_DIM, :])
            isc = isc + jnp.maximum(lg, 0.0) * wit_ref[h:h + 1, :]
        if on_diagonal:
            admissible = ((kb * tk + key_off) >> chunk_shift) <= q_chunk
            masked = jnp.where(admissible, isc, -jnp.inf)
        else:
            masked = isc
        isc_s[kb] = masked
        return (jnp.maximum(hi_part, _fold_rows(isc, jnp.max)),
                jnp.minimum(lo_part, _fold_rows(isc, jnp.min)),
                pos_part + _fold_rows(jnp.where(masked > 0.0, 1, 0), jnp.sum),
                nonneg_part + _fold_rows(jnp.where(masked >= 0.0, 1, 0), jnp.sum))

    carry = lax.fori_loop(
        0, n_kb - 1, functools.partial(score_block, on_diagonal=False),
        (jnp.full((ACC_ROWS, tq), -jnp.inf, F32), jnp.full((ACC_ROWS, tq), jnp.inf, F32),
         jnp.zeros((ACC_ROWS, tq), jnp.int32), jnp.zeros((ACC_ROWS, tq), jnp.int32)))
    hi_part, lo_part, pos_part, nonneg_part = score_block(n_kb - 1, carry, on_diagonal=True)
    s_max = jnp.max(hi_part, axis=0, keepdims=True)
    s_min = jnp.min(lo_part, axis=0, keepdims=True)
    n_pos = jnp.sum(pos_part, axis=0, keepdims=True)
    n_nonneg = jnp.sum(nonneg_part, axis=0, keepdims=True)

    def count_keys(pred_fn):
        def body(kb, part):
            hit = jnp.where(pred_fn(kb, isc_s[kb]), 1, 0)
            return part + _fold_rows(hit, jnp.sum)
        part = lax.fori_loop(0, n_kb, body, jnp.zeros((ACC_ROWS, tq), jnp.int32))
        return jnp.sum(part, axis=0, keepdims=True)

    q_pos = t0 + lax.broadcasted_iota(jnp.int32, (1, tq), 1)
    n_admissible = ((q_pos >> chunk_shift) + 1) << chunk_shift
    keep_all = n_admissible <= topk
    zero_thr = jnp.logical_and(jnp.logical_not(keep_all),
                               jnp.logical_and(n_pos < topk, n_nonneg >= topk))
    pinned = jnp.logical_or(keep_all, zero_thr)
    positive = n_pos >= topk
    pin = jnp.where(keep_all, s_min, 0.0)
    lo = jnp.where(pinned, pin, jnp.where(positive, 0.0, s_min))
    hi = jnp.where(pinned, pin, jnp.where(positive, s_max, 0.0))
    n_lo = jnp.where(jnp.logical_or(keep_all, jnp.logical_not(jnp.logical_or(zero_thr, positive))),
                     n_admissible, n_nonneg)

    def bisect_step(_, carry):
        lo, hi, n_lo = carry
        mid = lo + (hi - lo) * 0.5
        cnt = count_keys(lambda kb, blk: blk >= mid)
        keep = cnt >= topk
        return jnp.where(keep, mid, lo), jnp.where(keep, hi, mid), jnp.where(keep, cnt, n_lo)

    lo, _, n_lo = lax.fori_loop(0, BISECT_PASSES, bisect_step, (lo, hi, n_lo))
    thr_s[...] = jnp.broadcast_to(lo, thr_s.shape)
    unresolved = jnp.max(jnp.where(jnp.logical_or(zero_thr, n_lo <= topk), 0, 1)) > 0
    zero_ties = jnp.logical_and(zero_thr, n_nonneg > topk)
    need_s[...] = jnp.broadcast_to(jnp.where(zero_ties, topk - n_pos, seq), need_s.shape)
    tied = jnp.max(jnp.where(zero_ties, 1, 0)) > 0

    def to_bias(select_fn):
        def body(kb, carry):
            isc_s[kb] = jnp.where(select_fn(kb, isc_s[kb]), 0.0, NEG_INF)
            return carry
        lax.fori_loop(0, n_kb, body, 0)

    def to_bias_ranked(thr_v, need):
        lower_tri = jnp.where(lax.broadcasted_iota(jnp.int32, (tk, tk), 0)
                              >= lax.broadcasted_iota(jnp.int32, (tk, tk), 1), 1.0, 0.0).astype(BF16)
        need_f = need.astype(F32)

        def body(kb, before):
            blk = isc_s[kb]
            tie = blk == thr_v
            rank = before + _dot(lower_tri, jnp.where(tie, 1.0, 0.0).astype(BF16))
            keep = jnp.logical_or(blk > thr_v, jnp.logical_and(tie, rank <= need_f))
            isc_s[kb] = jnp.where(keep, 0.0, NEG_INF)
            return rank[tk - 1:tk, :]
        lax.fori_loop(0, n_kb, body, jnp.zeros((1, tq), F32))

    @pl.when(jnp.logical_not(jnp.logical_or(unresolved, tied)))
    def _():
        thr_v = thr_s[0:1, :]
        to_bias(lambda kb, blk: blk >= thr_v)

    @pl.when(jnp.logical_and(jnp.logical_not(unresolved), tied))
    def _():
        to_bias_ranked(thr_s[0:1, :], need_s[0:1, :])

    @pl.when(unresolved)
    def _():
        def bit_step(i, carry):
            thr_key, n_ge = carry
            cand = jnp.where(i == 0, jnp.zeros_like(thr_key), thr_key | (jnp.int32(1) << (31 - i)))
            cand_f = _key_to_f32(jnp.maximum(cand, KEY_NEG_FLT_MAX))
            cnt = count_keys(lambda kb, blk: blk >= cand_f)
            accept = cnt >= topk
            return jnp.where(accept, cand, thr_key), jnp.where(accept, cnt, n_ge)

        thr_key, n_ge = lax.fori_loop(
            0, 32, bit_step, (jnp.full((1, tq), INT_MIN, jnp.int32), jnp.zeros((1, tq), jnp.int32)))
        thr_v = _key_to_f32(jnp.maximum(thr_key, KEY_NEG_FLT_MAX))
        has_ties = jnp.max(n_ge) > topk

        @pl.when(jnp.logical_not(has_ties))
        def _():
            to_bias(lambda kb, blk: blk >= thr_v)

        @pl.when(has_ties)
        def _():
            n_gt = count_keys(lambda kb, blk: blk > thr_v)
            to_bias_ranked(thr_v, jnp.where(n_ge > topk, topk - n_gt, seq))

    zeros_half = jnp.zeros((HEAD_DIM, tq), BF16)
    for h in range(N_HEADS):
        head_rows = qt_ref[h * HEAD_DIM:(h + 1) * HEAD_DIM, :]
        pair = [head_rows, zeros_half] if h % 2 == 0 else [zeros_half, head_rows]
        qz_s[h] = jnp.concatenate(pair, axis=0)
    m_s[...] = jnp.full(m_s.shape, M_FLOOR, F32)
    acc_s[...] = jnp.zeros(acc_s.shape, F32)

    chunks = [slice(c, c + SOFTMAX_ROWS) for c in range(0, tk, SOFTMAX_ROWS)]

    def scores(kb, h):
        rows = pl.ds(pl.multiple_of(kb * tk, tk), tk)
        slab = slice((h // 2) * LANES, (h // 2 + 1) * LANES)
        s = _dot(k_ref[rows, slab], qz_s[h]) + isc_s[kb]
        s_st[h] = s
        mblk_s[h] = jnp.broadcast_to(
            jnp.max(_fold_rows(s, jnp.max), axis=0, keepdims=True), (SUBLANES, tq))

    def probs(h):
        m_old = m_s[h]
        m_new = jnp.maximum(m_old, mblk_s[h])
        alpha_s[h] = jnp.exp2(m_old - m_new)
        for c in chunks:
            p_st[h, c, :] = jnp.exp2(s_st[h, c, :] - m_new[0:1, :]).astype(BF16)
        m_s[h] = m_new

    def accum(kb, h):
        pv = _dot(vt_ref[kb, h * VT_ROWS:(h + 1) * VT_ROWS, :], p_st[h])
        acc_s[h] = alpha_s[h][0:1, :] * acc_s[h] + pv

    for h in range(N_HEADS):
        scores(0, h)

    @pl.when(n_kb < 2)
    def _():
        for h in range(N_HEADS):
            probs(h)

    @pl.when(n_kb >= 2)
    def _():
        for h in range(N_HEADS):
            probs(h)
        for h in range(N_HEADS):
            scores(1, h)

    def steady(t, carry):
        for h in range(N_HEADS):
            accum(t - 2, h)
        for h in range(N_HEADS):
            probs(h)
        for h in range(N_HEADS):
            scores(t, h)
        return carry

    lax.fori_loop(2, n_kb, steady, 0)

    @pl.when(n_kb >= 2)
    def _():
        for h in range(N_HEADS):
            accum(n_kb - 2, h)
        for h in range(N_HEADS):
            probs(h)

    for h in range(N_HEADS):
        accum(n_kb - 1, h)

    outs = []
    for h in range(N_HEADS):
        a = acc_s[h]
        outs.append(a[:HEAD_DIM] / a[HEAD_DIM:HEAD_DIM + 1])
    o_ref[...] = jnp.concatenate(outs, axis=0).T.astype(BF16)


def _dsa(qit, wit, ki, qt, k, vt, *, batch, seq, tq, tk):
    assert tk % tq == 0 and tq % CHUNK == 0 and seq % tk == 0, (seq, tq, tk)
    n = batch * seq
    nt = seq // tq
    topk = min(TOPK_MAX, seq // 4)
    n_blk = seq // tk
    qcol = lambda b, j: (b, j)
    whole = lambda b, j: (b, 0)
    kern = functools.partial(_dsa_kernel, tq=tq, tk=tk, topk=topk, seq=seq)
    return pl.pallas_call(
        kern,
        grid=(batch, nt),
        in_specs=[
            pl.BlockSpec((D_ATTN, tq), qcol),
            pl.BlockSpec((N_IDX_HEADS, tq), qcol),
            pl.BlockSpec((seq, IDX_DIM), whole),
            pl.BlockSpec((D_ATTN, tq), qcol),
            pl.BlockSpec((seq, D_ATTN), whole),
            pl.BlockSpec((None, n_blk, N_HEADS * VT_ROWS, tk), lambda b, j: (b, 0, 0, 0)),
        ],
        out_specs=pl.BlockSpec((tq, D_ATTN), lambda b, j: (b * nt + j, 0)),
        out_shape=jax.ShapeDtypeStruct((n, D_ATTN), BF16),
        scratch_shapes=[
            pltpu.VMEM((n_blk, tk, tq), F32),
            pltpu.VMEM((N_HEADS, LANES, tq), BF16),
            pltpu.VMEM((SUBLANES, tq), F32),
            pltpu.VMEM((SUBLANES, tq), jnp.int32),
            pltpu.VMEM((N_HEADS, SUBLANES, tq), F32),
            pltpu.VMEM((N_HEADS, SUBLANES, tq), F32),
            pltpu.VMEM((N_HEADS, SUBLANES, tq), F32),
            pltpu.VMEM((N_HEADS, VT_ROWS, tq), F32),
            pltpu.VMEM((N_HEADS, tk, tq), F32),
            pltpu.VMEM((N_HEADS, tk, tq), BF16),
        ],
        compiler_params=pltpu.CompilerParams(
            dimension_semantics=("arbitrary", "arbitrary"), vmem_limit_bytes=VMEM_LIMIT),
        name="dsa",
    )(qit, wit, ki, qt, k, vt)


def _mixer_kernel(x_ref, xh_ref, at_ref, wab_ref, dww_ref, dwb_ref, cg_ref, cb_ref,
                  wg_ref, gb_ref, wco_ref, wao_ref, wout_ref, g_ref, b_ref, o_ref,
                  u_s, ush_s, c_s, *, tm):
    j = pl.program_id(1)
    x = x_ref[...]
    xb = x.astype(BF16)

    def glu(rows_bf16):
        ab = _dot(rows_bf16, wab_ref[...])
        return ab[:, :D_CONV] * jax.nn.sigmoid(ab[:, D_CONV:])

    u_halo = glu(xh_ref[...].astype(BF16))
    u_s[0:CONV_HALO, :] = jnp.where(j > 0, u_halo, 0.0)
    u_s[CONV_HALO:CONV_HALO + tm, :] = glu(xb)

    gate_c = jax.nn.sigmoid(_dot(xb, wg_ref[:, :D_MODEL]) + gb_ref[:, :D_MODEL])
    gate_a = jax.nn.sigmoid(_dot(xb, wg_ref[:, D_MODEL:]) + gb_ref[:, D_MODEL:])
    attn_part = gate_a * _dot(at_ref[...], wao_ref[...])

    shifted_rows = CONV_HALO + tm - SUBLANES
    for r in range(1, SUBLANES):
        ush_s[r - 1, 0:shifted_rows, :] = u_s[r:r + shifted_rows, :]

    rows_per_step = 64
    first = CONV_HALO - (CONV_WIDTH - 1)
    for r0 in range(0, tm, rows_per_step):
        for c0 in range(0, D_CONV, LANES):
            acc = jnp.broadcast_to(dwb_ref[:, c0:c0 + LANES], (rows_per_step, LANES))
            for tap in range(CONV_WIDTH):
                phase = (first + tap) % SUBLANES
                base = r0 + (first + tap) - phase
                src = u_s if phase == 0 else ush_s.at[phase - 1]
                acc = acc + (src[base:base + rows_per_step, c0:c0 + LANES]
                             * dww_ref[tap:tap + 1, c0:c0 + LANES])
            c_s[r0:r0 + rows_per_step, c0:c0 + LANES] = acc

    y = _layer_norm(c_s[...], cg_ref[...], cb_ref[...])
    conv_feat = (y * jax.nn.sigmoid(y)).astype(BF16)

    merged = gate_c * _dot(conv_feat, wco_ref[...]) + attn_part
    mixer = _dot(merged.astype(BF16), wout_ref[...])
    o_ref[...] = _layer_norm(DEEPNORM_ALPHA * x + mixer, g_ref[...], b_ref[...])


def _mixer(x2, at, w_ab, dw_w, dw_b, conv_g, conv_b, w_g, gate_b, w_co, w_ao, w_out, ln_g, ln_b,
           *, batch, seq, tm):
    n = batch * seq
    nt = seq // tm
    halo_per_tile = tm // CONV_HALO
    halo_per_seq = seq // CONV_HALO
    row = lambda b, j: (b * nt + j, 0)
    halo = lambda b, j: (jnp.maximum(b * halo_per_seq + j * halo_per_tile - 1, 0), 0)
    const = lambda b, j: (0, 0)
    return pl.pallas_call(
        functools.partial(_mixer_kernel, tm=tm),
        grid=(batch, nt),
        in_specs=[
            pl.BlockSpec((tm, D_MODEL), row),
            pl.BlockSpec((CONV_HALO, D_MODEL), halo),
            pl.BlockSpec((tm, D_ATTN), row),
            pl.BlockSpec((D_MODEL, 2 * D_CONV), const),
            pl.BlockSpec((CONV_HALO, D_CONV), const),
            pl.BlockSpec((1, D_CONV), const),
            pl.BlockSpec((1, D_CONV), const),
            pl.BlockSpec((1, D_CONV), const),
            pl.BlockSpec((D_MODEL, 2 * D_MODEL), const),
            pl.BlockSpec((1, 2 * D_MODEL), const),
            pl.BlockSpec((D_CONV, D_MODEL), const),
            pl.BlockSpec((D_ATTN, D_MODEL), const),
            pl.BlockSpec((D_MODEL, D_MODEL), const),
            pl.BlockSpec((1, D_MODEL), const),
            pl.BlockSpec((1, D_MODEL), const),
        ],
        out_specs=pl.BlockSpec((tm, D_MODEL), row),
        out_shape=jax.ShapeDtypeStruct((n, D_MODEL), F32),
        scratch_shapes=[
            pltpu.VMEM((CONV_HALO + tm, D_CONV), F32),
            pltpu.VMEM((SUBLANES - 1, CONV_HALO + tm, D_CONV), F32),
            pltpu.VMEM((tm, D_CONV), F32),
        ],
        compiler_params=pltpu.CompilerParams(
            dimension_semantics=("arbitrary", "arbitrary"), vmem_limit_bytes=VMEM_LIMIT),
        name="mixer",
    )(x2, x2, at, w_ab, dw_w, dw_b, conv_g, conv_b, w_g, gate_b, w_co, w_ao, w_out, ln_g, ln_b)


def _ffn_kernel(x_ref, wi_ref, wo_ref, g_ref, b_ref, o_ref, *, ff_chunk):
    x = x_ref[...]
    xb = x.astype(BF16)
    acc = jnp.zeros(x.shape, F32)
    for c0 in range(0, D_FF, ff_chunk):
        h = jnp.maximum(_dot(xb, wi_ref[:, c0:c0 + ff_chunk]), 0.0)
        acc = acc + _dot((h * h).astype(BF16), wo_ref[c0:c0 + ff_chunk, :])
    o_ref[...] = _layer_norm(DEEPNORM_ALPHA * x + acc, g_ref[...], b_ref[...])


def _ffn(x2, w_i, w_o, ln_g, ln_b, *, tm, ff_chunk):
    n = x2.shape[0]
    row = lambda i: (i, 0)
    const = lambda i: (0, 0)
    return pl.pallas_call(
        functools.partial(_ffn_kernel, ff_chunk=ff_chunk),
        grid=(n // tm,),
        in_specs=[
            pl.BlockSpec((tm, D_MODEL), row),
            pl.BlockSpec((D_MODEL, D_FF), const, pipeline_mode=pl.Buffered(1)),
            pl.BlockSpec((D_FF, D_MODEL), const, pipeline_mode=pl.Buffered(1)),
            pl.BlockSpec((1, D_MODEL), const),
            pl.BlockSpec((1, D_MODEL), const),
        ],
        out_specs=pl.BlockSpec((tm, D_MODEL), row),
        out_shape=jax.ShapeDtypeStruct((n, D_MODEL), F32),
        compiler_params=pltpu.CompilerParams(
            dimension_semantics=("arbitrary",), vmem_limit_bytes=VMEM_LIMIT),
        name="ffn",
    )(x2, w_i, w_o, ln_g, ln_b)


def _rope_tables(seq):
    inv_freq = ROPE_THETA ** (-jnp.arange(0, HEAD_DIM, 2, dtype=F32) / HEAD_DIM)
    ang = jnp.arange(seq, dtype=jnp.int32).astype(F32)[:, None] * inv_freq[None, :]
    cos = jnp.cos(ang)
    sin = jnp.sin(ang)
    cos_blk = jnp.concatenate([cos, cos], axis=1)
    sin_blk = jnp.concatenate([-sin, sin], axis=1)
    return jnp.tile(cos_blk, (1, N_HEADS)), jnp.tile(sin_blk, (1, N_HEADS))


def _pad_lanes(v, width):
    return jnp.pad(v, (0, width - v.shape[0])).reshape(1, width)


def kernel(x, w_in, dw_w, dw_b, conv_ln_g, conv_ln_b, w_conv_out, idx_k_ln_g, idx_k_ln_b,
           w_attn_out, gate_b, w_out, ln1_g, ln1_b, w_ff_in, w_ff_out, ln2_g, ln2_b):
    batch, seq, _ = x.shape
    cos_t, sin_t = _rope_tables(seq)
    tm = min(512, seq)
    tq = min(256, seq)
    tk = min(512, seq)

    o_a, o_b, o_q, o_k, o_v = 0, D_CONV, 2 * D_CONV, 2 * D_CONV + D_ATTN, 2 * D_CONV + 2 * D_ATTN
    o_qi = o_v + D_ATTN
    o_ki = o_qi + N_IDX_HEADS * IDX_DIM
    o_wi = o_ki + IDX_DIM
    o_gc = o_wi + N_IDX_HEADS
    o_ga = o_gc + D_MODEL

    h = x.reshape(batch * seq, D_MODEL)
    for layer in range(w_in.shape[0]):
        w = w_in[layer]
        small = jnp.pad(w[:, o_ki:o_gc], ((0, 0), (0, LANES - (o_gc - o_ki))))
        w_attn = jnp.concatenate([w[:, o_q:o_ki], small], axis=1).astype(BF16)
        w_ab = w[:, o_a:o_q].astype(BF16)
        w_g = w[:, o_gc:o_ga + D_MODEL].astype(BF16)

        qt, k, vt, qit, ki, wit = _inproj(
            h, w_attn, cos_t, sin_t,
            _pad_lanes(idx_k_ln_g[layer], LANES), _pad_lanes(idx_k_ln_b[layer], LANES),
            batch=batch, seq=seq, tm=tk)
        attn = _dsa(qit, wit, ki, qt, k, vt, batch=batch, seq=seq, tq=tq, tk=tk)

        dww = jnp.pad(dw_w[layer], ((0, CONV_HALO - CONV_WIDTH), (0, 0)))
        h = _mixer(h, attn, w_ab, dww, dw_b[layer].reshape(1, D_CONV),
                   conv_ln_g[layer].reshape(1, D_CONV), conv_ln_b[layer].reshape(1, D_CONV),
                   w_g, gate_b[layer].reshape(1, 2 * D_MODEL),
                   w_conv_out[layer].astype(BF16), w_attn_out[layer].astype(BF16),
                   w_out[layer].astype(BF16),
                   ln1_g[layer].reshape(1, D_MODEL), ln1_b[layer].reshape(1, D_MODEL),
                   batch=batch, seq=seq, tm=tm)
        h = _ffn(h, w_ff_in[layer].astype(BF16), w_ff_out[layer].astype(BF16),
                 ln2_g[layer].reshape(1, D_MODEL), ln2_b[layer].reshape(1, D_MODEL),
                 tm=tm, ff_chunk=1024)
    return h.reshape(batch, seq, D_MODEL)
```

```python
import functools

import jax
import jax.numpy as jnp
from jax import lax
from jax.experimental import pallas as pl
from jax.experimental.pallas import tpu as pltpu

D_MODEL = 1024
CHUNK = 64
D_CONV = 512
CONV_WIDTH = 31
N_HEADS = 8
HEAD_DIM = 64
D_ATTN = N_HEADS * HEAD_DIM
N_IDX_HEADS = 8
IDX_DIM = 64
TOPK_MAX = 256
D_FF = 4 * D_MODEL
ROPE_THETA = 10000.0
LN_EPS = 1e-5
NEG_INF = -1e30
DEPTH = 1
DEEPNORM_ALPHA = (2.0 * DEPTH) ** 0.25

LANES = 128
SUBLANES = 8
CONV_HALO = 32
ONES_ROWS = 16
VT_ROWS = HEAD_DIM + ONES_ROWS
ACC_ROWS = 4 * SUBLANES
BISECT_PASSES = 25
SOFTMAX_ROWS = 128
LOG2_E = 1.4426950408889634
INT_MIN = -(2 ** 31)
KEY_NEG_FLT_MAX = -(2 ** 31) + 0x00800000
M_FLOOR = -1e20
VMEM_LIMIT = 56 * 1024 * 1024

F32 = jnp.float32
BF16 = jnp.bfloat16


def _dot(a, b):
    return jnp.dot(a, b, preferred_element_type=F32)


def _layer_norm(x, g, b):
    mu = jnp.mean(x, axis=-1, keepdims=True)
    d = x - mu
    var = jnp.mean(d * d, axis=-1, keepdims=True)
    return d * lax.rsqrt(var + LN_EPS) * g + b


def _rope(p, cos, sin_signed):
    width = p.shape[-1]
    lane = lax.broadcasted_iota(jnp.int32, p.shape, 1)
    first_half = (lane % HEAD_DIM) < (HEAD_DIM // 2)
    swapped = jnp.where(first_half,
                        pltpu.roll(p, width - HEAD_DIM // 2, 1),
                        pltpu.roll(p, HEAD_DIM // 2, 1))
    return p * cos + swapped * sin_signed


def _rope_transposed(pt, cos_t, sin_signed_t):
    half = HEAD_DIM // 2
    blocks = []
    for r0 in range(0, pt.shape[0], HEAD_DIM):
        blocks += [pt[r0 + half:r0 + HEAD_DIM], pt[r0:r0 + half]]
    return pt * cos_t + jnp.concatenate(blocks, axis=0) * sin_signed_t


def _inproj_kernel(x_ref, w_ref, cos_ref, sin_ref, cost_ref, sint_ref, kig_ref, kib_ref,
                   qt_ref, k_ref, vt_ref, qit_ref, ki_ref, wit_ref):
    xb = x_ref[...].astype(BF16)
    tm = xb.shape[0]
    cos = cos_ref[...]
    sin = sin_ref[...]
    cos_t = cost_ref[...]
    sin_t = sint_ref[...]
    qt = _rope_transposed(_dot(xb, w_ref[:, 0:D_ATTN]).T, cos_t, sin_t)
    qt_ref[...] = (qt * (HEAD_DIM ** -0.5 * LOG2_E)).astype(BF16)
    k_ref[...] = _rope(_dot(xb, w_ref[:, D_ATTN:2 * D_ATTN]), cos, sin).astype(BF16)

    vt = _dot(xb, w_ref[:, 2 * D_ATTN:3 * D_ATTN]).T
    ones = jnp.ones((ONES_ROWS, tm), F32)
    pieces = []
    for h in range(N_HEADS):
        pieces += [vt[h * HEAD_DIM:(h + 1) * HEAD_DIM], ones]
    vt_ref[...] = jnp.concatenate(pieces, axis=0).astype(BF16)

    qit = _rope_transposed(_dot(xb, w_ref[:, 3 * D_ATTN:4 * D_ATTN]).T, cos_t, sin_t)
    qit_ref[...] = (qit * (IDX_DIM ** -0.5)).astype(BF16)

    sm = _dot(xb, w_ref[:, 4 * D_ATTN:4 * D_ATTN + LANES])
    lane = lax.broadcasted_iota(jnp.int32, sm.shape, 1)
    is_ki = lane < IDX_DIM
    mu = jnp.sum(jnp.where(is_ki, sm, 0.0), axis=-1, keepdims=True) * (1.0 / IDX_DIM)
    d = jnp.where(is_ki, sm - mu, 0.0)
    var = jnp.sum(d * d, axis=-1, keepdims=True) * (1.0 / IDX_DIM)
    kn = d * lax.rsqrt(var + LN_EPS) * kig_ref[...] + kib_ref[...]
    kr = _rope(kn, cos[:, :LANES], sin[:, :LANES])
    ki_ref[...] = kr[:, :IDX_DIM].astype(BF16)
    wit_ref[...] = sm.T[IDX_DIM:IDX_DIM + N_IDX_HEADS] * (N_IDX_HEADS ** -0.5)


def _inproj(x2, w_attn, cos_t, sin_t, kig, kib, *, batch, seq, tm):
    n = batch * seq
    nt = seq // tm
    row = lambda j, b: (b * nt + j, 0)
    col = lambda j, b: (b, j)
    pos = lambda j, b: (j, 0)
    const = lambda j, b: (0, 0)
    wcols = w_attn.shape[1]
    return pl.pallas_call(
        _inproj_kernel,
        grid=(nt, batch),
        in_specs=[
            pl.BlockSpec((tm, D_MODEL), row),
            pl.BlockSpec((D_MODEL, wcols), const),
            pl.BlockSpec((tm, D_ATTN), pos),
            pl.BlockSpec((tm, D_ATTN), pos),
            pl.BlockSpec((D_ATTN, tm), lambda j, b: (0, j)),
            pl.BlockSpec((D_ATTN, tm), lambda j, b: (0, j)),
            pl.BlockSpec((1, LANES), const),
            pl.BlockSpec((1, LANES), const),
        ],
        out_specs=[
            pl.BlockSpec((D_ATTN, tm), col),
            pl.BlockSpec((tm, D_ATTN), row),
            pl.BlockSpec((None, None, N_HEADS * VT_ROWS, tm), lambda j, b: (b, j, 0, 0)),
            pl.BlockSpec((D_ATTN, tm), col),
            pl.BlockSpec((tm, IDX_DIM), row),
            pl.BlockSpec((N_IDX_HEADS, tm), col),
        ],
        out_shape=[
            jax.ShapeDtypeStruct((batch * D_ATTN, seq), BF16),
            jax.ShapeDtypeStruct((n, D_ATTN), BF16),
            jax.ShapeDtypeStruct((batch, nt, N_HEADS * VT_ROWS, tm), BF16),
            jax.ShapeDtypeStruct((batch * D_ATTN, seq), BF16),
            jax.ShapeDtypeStruct((n, IDX_DIM), BF16),
            jax.ShapeDtypeStruct((batch * N_IDX_HEADS, seq), F32),
        ],
        compiler_params=pltpu.CompilerParams(
            dimension_semantics=("arbitrary", "arbitrary"), vmem_limit_bytes=VMEM_LIMIT),
        name="inproj",
    )(x2, w_attn, cos_t, sin_t, cos_t.T, sin_t.T, kig, kib)


def _key_to_f32(key):
    bits = key ^ ((key >> 31) & jnp.int32(0x7FFFFFFF))
    return lax.bitcast_convert_type(bits, F32)


def _fold_rows(x, op):
    rows, cols = x.shape
    return op(x.reshape(rows // ACC_ROWS, ACC_ROWS, cols), axis=0)


def _dsa_kernel(qit_ref, wit_ref, ki_ref, qt_ref, k_ref, vt_ref, o_ref,
                isc_s, qz_s, thr_s, need_s, m_s, mblk_s, alpha_s, acc_s, s_st, p_st, *, tq, tk, topk, seq):
    j = pl.program_id(1)
    t0 = j * tq
    n_kb = (t0 + tq + tk - 1) // tk

    chunk_shift = CHUNK.bit_length() - 1
    q_chunk = (t0 + lax.broadcasted_iota(jnp.int32, (tk, tq), 1)) >> chunk_shift
    key_off = lax.broadcasted_iota(jnp.int32, (tk, tq), 0)

    def score_block(kb, carry, on_diagonal):
        hi_part, lo_part, pos_part, nonneg_part = carry
        kib = ki_ref[pl.ds(pl.multiple_of(kb * tk, tk), tk), :]
        isc = jnp.zeros((tk, tq), F32)
        for h in range(N_IDX_HEADS):
            lg = _dot(kib, qit_ref[h * IDX_DIM:(h + 1) * IDX_DIM, :])
            isc = isc + jnp.maximum(lg, 0.0) * wit_ref[h:h + 1, :]
        if on_diagonal:
            admissible = ((kb * tk + key_off) >> chunk_shift) <= q_chunk
            masked = jnp.where(admissible, isc, -jnp.inf)
        else:
            masked = isc
        isc_s[kb] = masked
        return (jnp.maximum(hi_part, _fold_rows(isc, jnp.max)),
                jnp.minimum(lo_part, _fold_rows(isc, jnp.min)),
                pos_part + _fold_rows(jnp.where(masked > 0.0, 1, 0), jnp.sum),
                nonneg_part + _fold_rows(jnp.where(masked >= 0.0, 1, 0), jnp.sum))

    carry = lax.fori_loop(
        0, n_kb - 1, functools.partial(score_block, on_diagonal=False),
        (jnp.full((ACC_ROWS, tq), -jnp.inf, F32), jnp.full((ACC_ROWS, tq), jnp.inf, F32),
         jnp.zeros((ACC_ROWS, tq), jnp.int32), jnp.zeros((ACC_ROWS, tq), jnp.int32)))
    hi_part, lo_part, pos_part, nonneg_part = score_block(n_kb - 1, carry, on_diagonal=True)
    s_max = jnp.max(hi_part, axis=0, keepdims=True)
    s_min = jnp.min(lo_part, axis=0, keepdims=True)
    n_pos = jnp.sum(pos_part, axis=0, keepdims=True)
    n_nonneg = jnp.sum(nonneg_part, axis=0, keepdims=True)

    def count_keys(pred_fn):
        def body(kb, part):
            hit = jnp.where(pred_fn(kb, isc_s[kb]), 1, 0)
            return part + _fold_rows(hit, jnp.sum)
        part = lax.fori_loop(0, n_kb, body, jnp.zeros((ACC_ROWS, tq), jnp.int32))
        return jnp.sum(part, axis=0, keepdims=True)

    q_pos = t0 + lax.broadcasted_iota(jnp.int32, (1, tq), 1)
    n_admissible = ((q_pos >> chunk_shift) + 1) << chunk_shift
    keep_all = n_admissible <= topk
    zero_thr = jnp.logical_and(jnp.logical_not(keep_all),
                               jnp.logical_and(n_pos < topk, n_nonneg >= topk))
    pinned = jnp.logical_or(keep_all, zero_thr)
    positive = n_pos >= topk
    pin = jnp.where(keep_all, s_min, 0.0)
    lo = jnp.where(pinned, pin, jnp.where(positive, 0.0, s_min))
    hi = jnp.where(pinned, pin, jnp.where(positive, s_max, 0.0))
    n_lo = jnp.where(jnp.logical_or(keep_all, jnp.logical_not(jnp.logical_or(zero_thr, positive))),
                     n_admissible, n_nonneg)

    def bisect_step(_, carry):
        lo, hi, n_lo = carry
        mid = lo + (hi - lo) * 0.5
        cnt = count_keys(lambda kb, blk: blk >= mid)
        keep = cnt >= topk
        return jnp.where(keep, mid, lo), jnp.where(keep, hi, mid), jnp.where(keep, cnt, n_lo)

    lo, _, n_lo = lax.fori_loop(0, BISECT_PASSES, bisect_step, (lo, hi, n_lo))
    thr_s[...] = jnp.broadcast_to(lo, thr_s.shape)
    unresolved = jnp.max(jnp.where(jnp.logical_or(zero_thr, n_lo <= topk), 0, 1)) > 0
    zero_ties = jnp.logical_and(zero_thr, n_nonneg > topk)
    need_s[...] = jnp.broadcast_to(jnp.where(zero_ties, topk - n_pos, seq), need_s.shape)
    tied = jnp.max(jnp.where(zero_ties, 1, 0)) > 0

    def to_bias(select_fn):
        def body(kb, carry):
            isc_s[kb] = jnp.where(select_fn(kb, isc_s[kb]), 0.0, NEG_INF)
            return carry
        lax.fori_loop(0, n_kb, body, 0)

    def to_bias_ranked(thr_v, need):
        lower_tri = jnp.where(lax.broadcasted_iota(jnp.int32, (tk, tk), 0)
                              >= lax.broadcasted_iota(jnp.int32, (tk, tk), 1), 1.0, 0.0).astype(BF16)
        need_f = need.astype(F32)

        def body(kb, before):
            blk = isc_s[kb]
            tie = blk == thr_v
            rank = before + _dot(lower_tri, jnp.where(tie, 1.0, 0.0).astype(BF16))
            keep = jnp.logical_or(blk > thr_v, jnp.logical_and(tie, rank <= need_f))
            isc_s[kb] = jnp.where(keep, 0.0, NEG_INF)
            return rank[tk - 1:tk, :]
        lax.fori_loop(0, n_kb, body, jnp.zeros((1, tq), F32))

    @pl.when(jnp.logical_not(jnp.logical_or(unresolved, tied)))
    def _():
        thr_v = thr_s[0:1, :]
        to_bias(lambda kb, blk: blk >= thr_v)

    @pl.when(jnp.logical_and(jnp.logical_not(unresolved), tied))
    def _():
        to_bias_ranked(thr_s[0:1, :], need_s[0:1, :])

    @pl.when(unresolved)
    def _():
        def bit_step(i, carry):
            thr_key, n_ge = carry
            cand = jnp.where(i == 0, jnp.zeros_like(thr_key), thr_key | (jnp.int32(1) << (31 - i)))
            cand_f = _key_to_f32(jnp.maximum(cand, KEY_NEG_FLT_MAX))
            cnt = count_keys(lambda kb, blk: blk >= cand_f)
            accept = cnt >= topk
            return jnp.where(accept, cand, thr_key), jnp.where(accept, cnt, n_ge)

        thr_key, n_ge = lax.fori_loop(
            0, 32, bit_step, (jnp.full((1, tq), INT_MIN, jnp.int32), jnp.zeros((1, tq), jnp.int32)))
        thr_v = _key_to_f32(jnp.maximum(thr_key, KEY_NEG_FLT_MAX))
        has_ties = jnp.max(n_ge) > topk

        @pl.when(jnp.logical_not(has_ties))
        def _():
            to_bias(lambda kb, blk: blk >= thr_v)

        @pl.when(has_ties)
        def _():
            n_gt = count_keys(lambda kb, blk: blk > thr_v)
            to_bias_ranked(thr_v, jnp.where(n_ge > topk, topk - n_gt, seq))

    zeros_half = jnp.zeros((HEAD_DIM, tq), BF16)
    for h in range(N_HEADS):
        head_rows = qt_ref[h * HEAD_DIM:(h + 1) * HEAD_DIM, :]
        pair = [head_rows, zeros_half] if h % 2 == 0 else [zeros_half, head_rows]
        qz_s[h] = jnp.concatenate(pair, axis=0)
    m_s[...] = jnp.full(m_s.shape, M_FLOOR, F32)
    acc_s[...] = jnp.zeros(acc_s.shape, F32)

    chunks = [slice(c, c + SOFTMAX_ROWS) for c in range(0, tk, SOFTMAX_ROWS)]

    def scores(kb, h):
        rows = pl.ds(pl.multiple_of(kb * tk, tk), tk)
        slab = slice((h // 2) * LANES, (h // 2 + 1) * LANES)
        s = _dot(k_ref[rows, slab], qz_s[h]) + isc_s[kb]
        s_st[h] = s
        mblk_s[h] = jnp.broadcast_to(
            jnp.max(_fold_rows(s, jnp.max), axis=0, keepdims=True), (SUBLANES, tq))

    def probs(h):
        m_old = m_s[h]
        m_new = jnp.maximum(m_old, mblk_s[h])
        alpha_s[h] = jnp.exp2(m_old - m_new)
        for c in chunks:
            p_st[h, c, :] = jnp.exp2(s_st[h, c, :] - m_new[0:1, :]).astype(BF16)
        m_s[h] = m_new

    def accum(kb, h):
        pv = _dot(vt_ref[kb, h * VT_ROWS:(h + 1) * VT_ROWS, :], p_st[h])
        acc_s[h] = alpha_s[h][0:1, :] * acc_s[h] + pv

    for h in range(N_HEADS):
        scores(0, h)

    @pl.when(n_kb < 2)
    def _():
        for h in range(N_HEADS):
            probs(h)

    @pl.when(n_kb >= 2)
    def _():
        for h in range(N_HEADS):
            probs(h)
        for h in range(N_HEADS):
            scores(1, h)

    def steady(t, carry):
        for h in range(N_HEADS):
            accum(t - 2, h)
        for h in range(N_HEADS):
            probs(h)
        for h in range(N_HEADS):
            scores(t, h)
        return carry

    lax.fori_loop(2, n_kb, steady, 0)

    @pl.when(n_kb >= 2)
    def _():
        for h in range(N_HEADS):
            accum(n_kb - 2, h)
        for h in range(N_HEADS):
            probs(h)

    for h in range(N_HEADS):
        accum(n_kb - 1, h)

    outs = []
    for h in range(N_HEADS):
        a = acc_s[h]
        outs.append(a[:HEAD_DIM] / a[HEAD_DIM:HEAD_DIM + 1])
    o_ref[...] = jnp.concatenate(outs, axis=0).T.astype(BF16)


def _dsa(qit, wit, ki, qt, k, vt, *, batch, seq, tq, tk):
    assert tk % tq == 0 and tq % CHUNK == 0 and seq % tk == 0, (seq, tq, tk)
    n = batch * seq
    nt = seq // tq
    topk = min(TOPK_MAX, seq // 4)
    n_blk = seq // tk
    qcol = lambda b, j: (b, j)
    whole = lambda b, j: (b, 0)
    kern = functools.partial(_dsa_kernel, tq=tq, tk=tk, topk=topk, seq=seq)
    return pl.pallas_call(
        kern,
        grid=(batch, nt),
        in_specs=[
            pl.BlockSpec((D_ATTN, tq), qcol),
            pl.BlockSpec((N_IDX_HEADS, tq), qcol),
            pl.BlockSpec((seq, IDX_DIM), whole),
            pl.BlockSpec((D_ATTN, tq), qcol),
            pl.BlockSpec((seq, D_ATTN), whole),
            pl.BlockSpec((None, n_blk, N_HEADS * VT_ROWS, tk), lambda b, j: (b, 0, 0, 0)),
        ],
        out_specs=pl.BlockSpec((tq, D_ATTN), lambda b, j: (b * nt + j, 0)),
        out_shape=jax.ShapeDtypeStruct((n, D_ATTN), BF16),
        scratch_shapes=[
            pltpu.VMEM((n_blk, tk, tq), F32),
            pltpu.VMEM((N_HEADS, LANES, tq), BF16),
            pltpu.VMEM((SUBLANES, tq), F32),
            pltpu.VMEM((SUBLANES, tq), jnp.int32),
            pltpu.VMEM((N_HEADS, SUBLANES, tq), F32),
            pltpu.VMEM((N_HEADS, SUBLANES, tq), F32),
            pltpu.VMEM((N_HEADS, SUBLANES, tq), F32),
            pltpu.VMEM((N_HEADS, VT_ROWS, tq), F32),
            pltpu.VMEM((N_HEADS, tk, tq), F32),
            pltpu.VMEM((N_HEADS, tk, tq), BF16),
        ],
        compiler_params=pltpu.CompilerParams(
            dimension_semantics=("arbitrary", "arbitrary"), vmem_limit_bytes=VMEM_LIMIT),
        name="dsa",
    )(qit, wit, ki, qt, k, vt)


def _mixer_kernel(x_ref, xh_ref, at_ref, wab_ref, dww_ref, dwb_ref, cg_ref, cb_ref,
                  wg_ref, gb_ref, wco_ref, wao_ref, wout_ref, g_ref, b_ref,
                  wfi_ref, wfo_ref, g2_ref, b2_ref, o_ref,
                  u_s, ush_s, c_s, *, tm, ff_chunk):
    j = pl.program_id(1)
    x = x_ref[...]
    xb = x.astype(BF16)

    def glu(rows_bf16):
        ab = _dot(rows_bf16, wab_ref[...])
        return ab[:, :D_CONV] * jax.nn.sigmoid(ab[:, D_CONV:])

    u_halo = glu(xh_ref[...].astype(BF16))
    u_s[0:CONV_HALO, :] = jnp.where(j > 0, u_halo, 0.0)
    u_s[CONV_HALO:CONV_HALO + tm, :] = glu(xb)

    gate_c = jax.nn.sigmoid(_dot(xb, wg_ref[:, :D_MODEL]) + gb_ref[:, :D_MODEL])
    gate_a = jax.nn.sigmoid(_dot(xb, wg_ref[:, D_MODEL:]) + gb_ref[:, D_MODEL:])
    attn_part = gate_a * _dot(at_ref[...], wao_ref[...])

    shifted_rows = CONV_HALO + tm - SUBLANES
    for r in range(1, SUBLANES):
        ush_s[r - 1, 0:shifted_rows, :] = u_s[r:r + shifted_rows, :]

    rows_per_step = 64
    first = CONV_HALO - (CONV_WIDTH - 1)
    for r0 in range(0, tm, rows_per_step):
        for c0 in range(0, D_CONV, LANES):
            acc = jnp.broadcast_to(dwb_ref[:, c0:c0 + LANES], (rows_per_step, LANES))
            for tap in range(CONV_WIDTH):
                phase = (first + tap) % SUBLANES
                base = r0 + (first + tap) - phase
                src = u_s if phase == 0 else ush_s.at[phase - 1]
                acc = acc + (src[base:base + rows_per_step, c0:c0 + LANES]
                             * dww_ref[tap:tap + 1, c0:c0 + LANES])
            c_s[r0:r0 + rows_per_step, c0:c0 + LANES] = acc

    y = _layer_norm(c_s[...], cg_ref[...], cb_ref[...])
    conv_feat = (y * jax.nn.sigmoid(y)).astype(BF16)

    merged = gate_c * _dot(conv_feat, wco_ref[...]) + attn_part
    mixer = _dot(merged.astype(BF16), wout_ref[...])
    x1 = _layer_norm(DEEPNORM_ALPHA * x + mixer, g_ref[...], b_ref[...])

    x1b = x1.astype(BF16)
    acc = jnp.zeros(x1.shape, F32)
    for c0 in range(0, D_FF, ff_chunk):
        hid = jnp.maximum(_dot(x1b, wfi_ref[:, c0:c0 + ff_chunk]), 0.0)
        acc = acc + _dot((hid * hid).astype(BF16), wfo_ref[c0:c0 + ff_chunk, :])
    o_ref[...] = _layer_norm(DEEPNORM_ALPHA * x1 + acc, g2_ref[...], b2_ref[...])


def _mixer(x2, at, w_ab, dw_w, dw_b, conv_g, conv_b, w_g, gate_b, w_co, w_ao, w_out, ln_g, ln_b,
           w_fi, w_fo, ln2_g, ln2_b, *, batch, seq, tm, ff_chunk):
    n = batch * seq
    nt = seq // tm
    halo_per_tile = tm // CONV_HALO
    halo_per_seq = seq // CONV_HALO
    row = lambda b, j: (b * nt + j, 0)
    halo = lambda b, j: (jnp.maximum(b * halo_per_seq + j * halo_per_tile - 1, 0), 0)
    const = lambda b, j: (0, 0)
    once = pl.Buffered(1)
    return pl.pallas_call(
        functools.partial(_mixer_kernel, tm=tm, ff_chunk=ff_chunk),
        grid=(batch, nt),
        in_specs=[
            pl.BlockSpec((tm, D_MODEL), row),
            pl.BlockSpec((CONV_HALO, D_MODEL), halo),
            pl.BlockSpec((tm, D_ATTN), row),
            pl.BlockSpec((D_MODEL, 2 * D_CONV), const, pipeline_mode=once),
            pl.BlockSpec((CONV_HALO, D_CONV), const),
            pl.BlockSpec((1, D_CONV), const),
            pl.BlockSpec((1, D_CONV), const),
            pl.BlockSpec((1, D_CONV), const),
            pl.BlockSpec((D_MODEL, 2 * D_MODEL), const, pipeline_mode=once),
            pl.BlockSpec((1, 2 * D_MODEL), const),
            pl.BlockSpec((D_CONV, D_MODEL), const, pipeline_mode=once),
            pl.BlockSpec((D_ATTN, D_MODEL), const, pipeline_mode=once),
            pl.BlockSpec((D_MODEL, D_MODEL), const, pipeline_mode=once),
            pl.BlockSpec((1, D_MODEL), const),
            pl.BlockSpec((1, D_MODEL), const),
            pl.BlockSpec((D_MODEL, D_FF), const, pipeline_mode=once),
            pl.BlockSpec((D_FF, D_MODEL), const, pipeline_mode=once),
            pl.BlockSpec((1, D_MODEL), const),
            pl.BlockSpec((1, D_MODEL), const),
        ],
        out_specs=pl.BlockSpec((tm, D_MODEL), row),
        out_shape=jax.ShapeDtypeStruct((n, D_MODEL), F32),
        scratch_shapes=[
            pltpu.VMEM((CONV_HALO + tm, D_CONV), F32),
            pltpu.VMEM((SUBLANES - 1, CONV_HALO + tm, D_CONV), F32),
            pltpu.VMEM((tm, D_CONV), F32),
        ],
        compiler_params=pltpu.CompilerParams(
            dimension_semantics=("arbitrary", "arbitrary"), vmem_limit_bytes=VMEM_LIMIT),
        name="mixer",
    )(x2, x2, at, w_ab, dw_w, dw_b, conv_g, conv_b, w_g, gate_b, w_co, w_ao, w_out, ln_g, ln_b,
      w_fi, w_fo, ln2_g, ln2_b)


def _ffn_kernel(x_ref, wi_ref, wo_ref, g_ref, b_ref, o_ref, *, ff_chunk):
    x = x_ref[...]
    xb = x.astype(BF16)
    acc = jnp.zeros(x.shape, F32)
    for c0 in range(0, D_FF, ff_chunk):
        h = jnp.maximum(_dot(xb, wi_ref[:, c0:c0 + ff_chunk]), 0.0)
        acc = acc + _dot((h * h).astype(BF16), wo_ref[c0:c0 + ff_chunk, :])
    o_ref[...] = _layer_norm(DEEPNORM_ALPHA * x + acc, g_ref[...], b_ref[...])


def _ffn(x2, w_i, w_o, ln_g, ln_b, *, tm, ff_chunk):
    n = x2.shape[0]
    row = lambda i: (i, 0)
    const = lambda i: (0, 0)
    return pl.pallas_call(
        functools.partial(_ffn_kernel, ff_chunk=ff_chunk),
        grid=(n // tm,),
        in_specs=[
            pl.BlockSpec((tm, D_MODEL), row),
            pl.BlockSpec((D_MODEL, D_FF), const, pipeline_mode=pl.Buffered(1)),
            pl.BlockSpec((D_FF, D_MODEL), const, pipeline_mode=pl.Buffered(1)),
            pl.BlockSpec((1, D_MODEL), const),
            pl.BlockSpec((1, D_MODEL), const),
        ],
        out_specs=pl.BlockSpec((tm, D_MODEL), row),
        out_shape=jax.ShapeDtypeStruct((n, D_MODEL), F32),
        compiler_params=pltpu.CompilerParams(
            dimension_semantics=("arbitrary",), vmem_limit_bytes=VMEM_LIMIT),
        name="ffn",
    )(x2, w_i, w_o, ln_g, ln_b)


def _rope_tables(seq):
    inv_freq = ROPE_THETA ** (-jnp.arange(0, HEAD_DIM, 2, dtype=F32) / HEAD_DIM)
    ang = jnp.arange(seq, dtype=jnp.int32).astype(F32)[:, None] * inv_freq[None, :]
    cos = jnp.cos(ang)
    sin = jnp.sin(ang)
    cos_blk = jnp.concatenate([cos, cos], axis=1)
    sin_blk = jnp.concatenate([-sin, sin], axis=1)
    return jnp.tile(cos_blk, (1, N_HEADS)), jnp.tile(sin_blk, (1, N_HEADS))


def _pad_lanes(v, width):
    return jnp.pad(v, (0, width - v.shape[0])).reshape(1, width)


def kernel(x, w_in, dw_w, dw_b, conv_ln_g, conv_ln_b, w_conv_out, idx_k_ln_g, idx_k_ln_b,
           w_attn_out, gate_b, w_out, ln1_g, ln1_b, w_ff_in, w_ff_out, ln2_g, ln2_b):
    batch, seq, _ = x.shape
    cos_t, sin_t = _rope_tables(seq)
    tm = min(512, seq)
    tq = min(256, seq)
    tk = min(512, seq)

    o_a, o_b, o_q, o_k, o_v = 0, D_CONV, 2 * D_CONV, 2 * D_CONV + D_ATTN, 2 * D_CONV + 2 * D_ATTN
    o_qi = o_v + D_ATTN
    o_ki = o_qi + N_IDX_HEADS * IDX_DIM
    o_wi = o_ki + IDX_DIM
    o_gc = o_wi + N_IDX_HEADS
    o_ga = o_gc + D_MODEL

    h = x.reshape(batch * seq, D_MODEL)
    for layer in range(w_in.shape[0]):
        w = w_in[layer]
        small = jnp.pad(w[:, o_ki:o_gc], ((0, 0), (0, LANES - (o_gc - o_ki))))
        w_attn = jnp.concatenate([w[:, o_q:o_ki], small], axis=1).astype(BF16)
        w_ab = w[:, o_a:o_q].astype(BF16)
        w_g = w[:, o_gc:o_ga + D_MODEL].astype(BF16)

        qt, k, vt, qit, ki, wit = _inproj(
            h, w_attn, cos_t, sin_t,
            _pad_lanes(idx_k_ln_g[layer], LANES), _pad_lanes(idx_k_ln_b[layer], LANES),
            batch=batch, seq=seq, tm=tk)
        attn = _dsa(qit, wit, ki, qt, k, vt, batch=batch, seq=seq, tq=tq, tk=tk)

        dww = jnp.pad(dw_w[layer], ((0, CONV_HALO - CONV_WIDTH), (0, 0)))
        h = _mixer(h, attn, w_ab, dww, dw_b[layer].reshape(1, D_CONV),
                   conv_ln_g[layer].reshape(1, D_CONV), conv_ln_b[layer].reshape(1, D_CONV),
                   w_g, gate_b[layer].reshape(1, 2 * D_MODEL),
                   w_conv_out[layer].astype(BF16), w_attn_out[layer].astype(BF16),
                   w_out[layer].astype(BF16),
                   ln1_g[layer].reshape(1, D_MODEL), ln1_b[layer].reshape(1, D_MODEL),
                   w_ff_in[layer].astype(BF16), w_ff_out[layer].astype(BF16),
                   ln2_g[layer].reshape(1, D_MODEL), ln2_b[layer].reshape(1, D_MODEL),
                   batch=batch, seq=seq, tm=tm, ff_chunk=1024)
    return h.reshape(batch, seq, D_MODEL)
```
